```python
import jax, jax.numpy as jnp
from jax import lax
import numpy as np

D_MODEL = 2048
BATCH = 2
SEQ = 8192
DEPTH = 2

N_A_LAYERS = DEPTH // 2
N_B_LAYERS = DEPTH - N_A_LAYERS
EPS = 1e-6
ROPE_THETA = 500000.0
Q_BLOCK = 128
NEG = -1e30

MLA_HEADS = 16
MLA_Q_LORA = 512
MLA_KV_LORA = 512
MLA_NOPE = 128
MLA_ROPE = 64
MLA_QK = MLA_NOPE + MLA_ROPE
MLA_V = 128
MLA_WIDTH = MLA_HEADS * MLA_V
MLA_IN = MLA_Q_LORA + MLA_KV_LORA + MLA_ROPE + MLA_WIDTH

SWA_HEADS = 32
SWA_KV_HEADS = 8
SWA_GROUP = SWA_HEADS // SWA_KV_HEADS
SWA_HEAD_DIM = 64
SWA_ROPE = SWA_HEAD_DIM // 4
SWA_WINDOW = 128
SWA_WIDTH = SWA_HEADS * SWA_HEAD_DIM
SWA_KV_WIDTH = SWA_KV_HEADS * SWA_HEAD_DIM
SWA_IN = 2 * SWA_WIDTH

kernel_name = 'yoco_mla_swa_sink_hybrid'


def rms_norm(x, g):
    xf = x.astype(jnp.float32)
    y = xf * lax.rsqrt(jnp.mean(xf * xf, axis=-1, keepdims=True) + EPS)
    return (y * g.astype(jnp.float32)).astype(x.dtype)


def rope_tables(positions, rot_dim):
    inv = jnp.power(jnp.float32(ROPE_THETA), -jnp.arange(0, rot_dim, 2, dtype=jnp.float32) / rot_dim)
    ang = positions.astype(jnp.float32)[..., None] * inv
    return jnp.cos(ang)[:, :, None, :], jnp.sin(ang)[:, :, None, :]


def apply_rope(x, cos, sin):
    half = cos.shape[-1]
    r = 2 * half
    x1 = x[..., :half].astype(jnp.float32)
    x2 = x[..., half:r].astype(jnp.float32)
    rot = jnp.concatenate([x1 * cos - x2 * sin, x2 * cos + x1 * sin], axis=-1).astype(x.dtype)
    return jnp.concatenate([rot, x[..., r:]], axis=-1)


def dense_causal_attention(q, k, v):
    B, S, H, Dq = q.shape
    nb = S // Q_BLOCK
    scale = Dq ** -0.5
    kpos = jnp.arange(S)

    def block(i):
        start = i * Q_BLOCK
        qb = lax.dynamic_slice_in_dim(q, start, Q_BLOCK, axis=1)
        s = jnp.einsum('bqhd,bkhd->bhqk', qb, k, preferred_element_type=jnp.float32) * scale
        qpos = start + jnp.arange(Q_BLOCK)
        s = jnp.where(kpos[None, :] <= qpos[:, None], s, NEG)
        p = jax.nn.softmax(s, axis=-1).astype(v.dtype)
        return jnp.einsum('bhqk,bkhd->bqhd', p, v)

    out = lax.map(block, jnp.arange(nb))
    return jnp.moveaxis(out, 0, 1).reshape(B, S, H, v.shape[-1])


def mla_mixer(h, cos, sin, w_in, g_qa, w_q_up, g_kva, w_kv_up, g_qn, g_kn, w_o):
    B, S, _ = h.shape
    proj = h @ w_in
    c_q, c_kv, k_rope, gate = jnp.split(
        proj, [MLA_Q_LORA, MLA_Q_LORA + MLA_KV_LORA, MLA_Q_LORA + MLA_KV_LORA + MLA_ROPE], axis=-1)
    q = (rms_norm(c_q, g_qa) @ w_q_up).reshape(B, S, MLA_HEADS, MLA_QK)
    kv = (rms_norm(c_kv, g_kva) @ w_kv_up).reshape(B, S, MLA_HEADS, MLA_NOPE + MLA_V)
    k_nope, v = jnp.split(kv, [MLA_NOPE], axis=-1)
    k = jnp.concatenate(
        [k_nope, jnp.broadcast_to(k_rope[:, :, None, :], (B, S, MLA_HEADS, MLA_ROPE))], axis=-1)
    q = rms_norm(q, g_qn)
    k = rms_norm(k, g_kn)
    q = jnp.concatenate([q[..., :MLA_NOPE], apply_rope(q[..., MLA_NOPE:], cos, sin)], axis=-1)
    k = jnp.concatenate([k[..., :MLA_NOPE], apply_rope(k[..., MLA_NOPE:], cos, sin)], axis=-1)
    o = dense_causal_attention(q, k, v).reshape(B, S, MLA_WIDTH)
    return (o * jax.nn.silu(gate)) @ w_o


def shared_kv(h, g_norm, w_kv, g_kn, cos, sin):
    B, S, _ = h.shape
    nb = S // SWA_WINDOW
    kv = rms_norm(h, g_norm) @ w_kv
    k, v = jnp.split(kv, [SWA_KV_WIDTH], axis=-1)
    k = apply_rope(rms_norm(k.reshape(B, S, SWA_KV_HEADS, SWA_HEAD_DIM), g_kn), cos, sin)
    v = v.reshape(B, S, SWA_KV_HEADS, SWA_HEAD_DIM)
    kb = k.reshape(B, nb, SWA_WINDOW, SWA_KV_HEADS, SWA_HEAD_DIM)
    vb = v.reshape(B, nb, SWA_WINDOW, SWA_KV_HEADS, SWA_HEAD_DIM)
    prev = lambda t: jnp.concatenate([jnp.zeros_like(t[:, :1]), t[:, :-1]], axis=1)
    return (jnp.concatenate([prev(kb), kb], axis=2), jnp.concatenate([prev(vb), vb], axis=2))


def swa_mixer(h, k_band, v_band, cos, sin, w_in, g_qn, sinks, w_o):
    B, S, _ = h.shape
    nb = S // SWA_WINDOW
    W = SWA_WINDOW
    q, gate = jnp.split(h @ w_in, [SWA_WIDTH], axis=-1)
    q = apply_rope(rms_norm(q.reshape(B, S, SWA_HEADS, SWA_HEAD_DIM), g_qn), cos, sin)
    qb = q.reshape(B, nb, W, SWA_KV_HEADS, SWA_GROUP, SWA_HEAD_DIM)
    s = jnp.einsum('bnqkgd,bnjkd->bnkgqj', qb, k_band,
                   preferred_element_type=jnp.float32) * (SWA_HEAD_DIM ** -0.5)
    qi = jnp.arange(W)[:, None]
    kj = jnp.arange(2 * W)[None, :] - W
    band = (kj <= qi) & (kj > qi - SWA_WINDOW)
    valid = band[None] & ((jnp.arange(nb)[:, None, None] > 0) | (kj >= 0)[None])
    s = jnp.where(valid[None, :, None, None], s, NEG)
    sink = sinks.astype(jnp.float32).reshape(SWA_KV_HEADS, SWA_GROUP)[None, None, :, :, None, None]
    m = jnp.maximum(jnp.max(s, axis=-1, keepdims=True), sink)
    e = jnp.exp(s - m)
    p = (e / (jnp.sum(e, axis=-1, keepdims=True) + jnp.exp(sink - m))).astype(v_band.dtype)
    o = jnp.einsum('bnkgqj,bnjkd->bnqkgd', p, v_band).reshape(B, S, SWA_WIDTH)
    return (o * jax.nn.silu(gate)) @ w_o


def setup_inputs(seed: int = 0) -> dict:
    key = jax.random.key(seed)
    ks = jax.random.split(key, 24)
    f32 = jnp.float32

    def w(k, shape, fan_in):
        return jax.random.normal(k, shape, f32) * (fan_in ** -0.5)

    def gain(k, shape):
        return 1.0 + 0.02 * jax.random.normal(k, shape, f32)

    nA, nB = N_A_LAYERS, N_B_LAYERS
    x = jax.random.normal(ks[0], (BATCH, SEQ, D_MODEL), f32)
    offsets = jax.random.randint(ks[1], (BATCH, 1), 0, 4096, dtype=jnp.int32)
    positions = offsets + jnp.arange(SEQ, dtype=jnp.int32)[None, :]
    return {
        'x': x,
        'positions': positions,
        'a_norm': gain(ks[2], (nA, D_MODEL)),
        'a_w_in': w(ks[3], (nA, D_MODEL, MLA_IN), D_MODEL),
        'a_q_latent_norm': gain(ks[4], (nA, MLA_Q_LORA)),
        'a_w_q_up': w(ks[5], (nA, MLA_Q_LORA, MLA_HEADS * MLA_QK), MLA_Q_LORA),
        'a_kv_latent_norm': gain(ks[6], (nA, MLA_KV_LORA)),
        'a_w_kv_up': w(ks[7], (nA, MLA_KV_LORA, MLA_HEADS * (MLA_NOPE + MLA_V)), MLA_KV_LORA),
        'a_q_head_norm': gain(ks[8], (nA, MLA_QK)),
        'a_k_head_norm': gain(ks[9], (nA, MLA_QK)),
        'a_w_o': w(ks[10], (nA, MLA_WIDTH, D_MODEL), MLA_WIDTH),
        'kv_norm': gain(ks[11], (D_MODEL,)),
        'kv_w': w(ks[12], (D_MODEL, 2 * SWA_KV_WIDTH), D_MODEL),
        'kv_k_head_norm': gain(ks[13], (SWA_HEAD_DIM,)),
        'b_norm': gain(ks[14], (nB, D_MODEL)),
        'b_w_in': w(ks[15], (nB, D_MODEL, SWA_IN), D_MODEL),
        'b_q_head_norm': gain(ks[16], (nB, SWA_HEAD_DIM)),
        'b_sinks': 0.5 * jax.random.normal(ks[17], (nB, SWA_HEADS), f32),
        'b_w_o': w(ks[18], (nB, SWA_WIDTH, D_MODEL), SWA_WIDTH),
    }


def reference(x, positions, a_norm, a_w_in, a_q_latent_norm, a_w_q_up, a_kv_latent_norm,
              a_w_kv_up, a_q_head_norm, a_k_head_norm, a_w_o, kv_norm, kv_w, kv_k_head_norm,
              b_norm, b_w_in, b_q_head_norm, b_sinks, b_w_o):
    cos_a, sin_a = rope_tables(positions, MLA_ROPE)
    cos_b, sin_b = rope_tables(positions, SWA_ROPE)
    k_band, v_band = None, None
    for layer in range(DEPTH):
        if layer < N_A_LAYERS:
            i = layer
            x = x + mla_mixer(rms_norm(x, a_norm[i]), cos_a, sin_a, a_w_in[i], a_q_latent_norm[i],
                              a_w_q_up[i], a_kv_latent_norm[i], a_w_kv_up[i],
                              a_q_head_norm[i], a_k_head_norm[i], a_w_o[i])
            if layer == N_A_LAYERS - 1:
                k_band, v_band = shared_kv(x, kv_norm, kv_w, kv_k_head_norm, cos_b, sin_b)
        else:
            j = layer - N_A_LAYERS
            x = x + swa_mixer(rms_norm(x, b_norm[j]), k_band, v_band, cos_b, sin_b,
                              b_w_in[j], b_q_head_norm[j], b_sinks[j], b_w_o[j])
    return x
```

```python
import functools
import math

import jax
import jax.numpy as jnp
from jax import lax
from jax.experimental import pallas as pl
from jax.experimental.pallas import tpu as pltpu

F32 = jnp.float32
BF16 = jnp.bfloat16

EPS = 1e-6
ROPE_THETA = 500000.0
NEG = -1e30
LOG2E = math.log2(math.e)

A_HEADS = 16
A_LORA = 512
A_NOPE = 128
A_ROPE = 64
A_QK = A_NOPE + A_ROPE
A_QK_PAD = 256
A_V = 128

B_HEADS = 32
B_KV_HEADS = 8
B_GROUP = B_HEADS // B_KV_HEADS
B_HD = 64
B_ROPE_HALF = 8
B_WINDOW = 128
B_K_PAD = 128

VMEM_LIMIT = 56 * 1024 * 1024

TM_IN = 512
TM_UP = 256
TQ_A = 512
TK_A = 512
TM_OUT = 256
TM_BIN = 256
QB_SWA = 512


def _const_spec(shape):
    return pl.BlockSpec(shape, lambda *_: (0,) * len(shape))


def _params(n_axes):
    return pltpu.CompilerParams(
        dimension_semantics=("arbitrary",) * n_axes, vmem_limit_bytes=VMEM_LIMIT)


def _rms_rows(xf, g_row):
    ms = jnp.mean(xf * xf, axis=-1, keepdims=True)
    return xf * lax.rsqrt(ms + EPS) * g_row


def _dot(a, b):
    return jnp.dot(a, b, preferred_element_type=F32)


def _dot_nt(a, b):
    return lax.dot_general(a, b, (((1,), (1,)), ((), ())), preferred_element_type=F32)


def _silu(g):
    return g * jax.nn.sigmoid(g)


def _mla_in_kernel(x_ref, g_ref, wl_ref, wg_ref, gq_ref, gkv_ref,
                   cq_ref, ckv_ref, kr_ref, gate_ref):
    h = _rms_rows(x_ref[...], g_ref[...]).astype(BF16)
    lat = _dot(h, wl_ref[...])
    cq_ref[...] = _rms_rows(lat[:, :A_LORA], gq_ref[...]).astype(BF16)
    ckv_ref[...] = _rms_rows(lat[:, A_LORA:2 * A_LORA], gkv_ref[...]).astype(BF16)
    kr_ref[...] = lat[:, 2 * A_LORA:]
    n_chunk = 512
    for c in range(gate_ref.shape[1] // n_chunk):
        sl = slice(c * n_chunk, (c + 1) * n_chunk)
        gate_ref[:, sl] = _silu(_dot(h, wg_ref[:, sl])).astype(BF16)


def _swap_half_64(t, lane):
    return jnp.where(lane < 32, pltpu.roll(t, 96, axis=1), pltpu.roll(t, 32, axis=1))


def _mla_up_kernel(cq_ref, ckv_ref, kr_ref, pos_ref, inv_ref, sign_ref,
                   wq_ref, wk_ref, wvt_ref, gq_ref, gkn_ref, gkr_ref,
                   q_ref, k_ref, vt_ref, *, q_scale):
    tm = cq_ref.shape[0]
    cq = cq_ref[...]
    ckv = ckv_ref[...]
    ang = pos_ref[...] * inv_ref[...]
    cos_t = jnp.cos(ang)
    sin_t = jnp.sin(ang) * sign_ref[...]
    lane = lax.broadcasted_iota(jnp.int32, (tm, 128), 1)

    def rope(t):
        return t * cos_t + _swap_half_64(t, lane) * sin_t

    gq = gq_ref[...]
    for hh in range(A_HEADS):
        qh = _dot(cq, wq_ref[:, hh * A_QK_PAD:(hh + 1) * A_QK_PAD])
        ss = jnp.sum(qh * qh, axis=-1, keepdims=True)
        r = lax.rsqrt(ss * (1.0 / A_QK) + EPS) * q_scale
        qn = qh * gq
        q_ref[0, hh, :, :A_NOPE] = (qn[:, :A_NOPE] * r).astype(BF16)
        q_ref[0, hh, :, A_NOPE:] = (rope(qn[:, A_NOPE:]) * r).astype(BF16)

    kr = kr_ref[...]
    ss_r = jnp.sum(kr * kr, axis=-1, keepdims=True)
    kr_rot = rope(kr * gkr_ref[...])
    gkn = gkn_ref[...]
    for hh in range(A_HEADS):
        kn = _dot(ckv, wk_ref[:, hh * A_NOPE:(hh + 1) * A_NOPE])
        ss = jnp.sum(kn * kn, axis=-1, keepdims=True) + ss_r
        r = lax.rsqrt(ss * (1.0 / A_QK) + EPS)
        k_ref[0, hh, :, :A_NOPE] = (kn * gkn * r).astype(BF16)
        k_ref[0, hh, :, A_NOPE:] = (kr_rot * r).astype(BF16)

    for c in range(4):
        rows = slice(c * 512, (c + 1) * 512)
        vt = _dot_nt(wvt_ref[rows, :], ckv)
        vt_ref[0, c * 4:(c + 1) * 4] = vt.reshape(4, A_V, tm).astype(BF16)


def _mla_attn_kernel(q_ref, k_ref, vt_ref, gate_ref, o_ref, *, tq, tk):
    seq = q_ref.shape[2]

    def q_body(qi, _):
        qs = pl.multiple_of(qi * tq, tq)
        q = q_ref[0, 0, pl.ds(qs, tq), :]

        def update(carry, st, ks, size):
            m, l, acc = carry
            m_new = jnp.maximum(m, jnp.max(st, axis=0, keepdims=True))
            p = jnp.exp2(st - m_new)
            alpha = jnp.exp2(m - m_new)
            l = alpha * l + jnp.sum(p, axis=0, keepdims=True)
            vt = vt_ref[0, 0, :, pl.ds(ks, size)]
            acc = alpha * acc + _dot(vt, p.astype(BF16))
            return m_new, l, acc

        def k_body(ki, carry):
            ks = pl.multiple_of(ki * tk, tk)
            st = _dot_nt(k_ref[0, 0, pl.ds(ks, tk), :], q)
            return update(carry, st, ks, tk)

        init = (jnp.full((1, tq), NEG, F32), jnp.zeros((1, tq), F32), jnp.zeros((A_V, tq), F32))
        carry = lax.fori_loop(0, (qi * tq) // tk, k_body, init)
        st = _dot_nt(k_ref[0, 0, pl.ds(qs, tq), :], q)
        row = lax.broadcasted_iota(jnp.int32, (tq, tq), 0)
        col = lax.broadcasted_iota(jnp.int32, (tq, tq), 1)
        st = jnp.where(row <= col, st, NEG)
        _, l, acc = update(carry, st, qs, tq)
        o = (acc * (1.0 / l)).T
        o_ref[pl.ds(qs, tq), :] = (o * gate_ref[pl.ds(qs, tq), :].astype(F32)).astype(BF16)
        return 0

    lax.fori_loop(0, seq // tq, q_body, 0)


def _mla_out_kernel(og_ref, x_ref, wo_ref, gkv_ref, wkt_ref, wvt_ref, pos_ref, inv_ref, gk_ref,
                    x1_ref, k_ref, vt_ref):
    tm = x_ref.shape[0]
    x1 = x_ref[...] + _dot(og_ref[...], wo_ref[...])
    x1_ref[...] = x1
    hk = _rms_rows(x1, gkv_ref[...]).astype(BF16)
    kt = _dot_nt(wkt_ref[...], hk).reshape(B_KV_HEADS, B_HD, tm)
    ss = jnp.sum(kt * kt, axis=1, keepdims=True)
    kn = kt * lax.rsqrt(ss * (1.0 / B_HD) + EPS) * gk_ref[...][None]
    ang = inv_ref[...] * pos_ref[0]
    c = jnp.cos(ang)[None]
    s = jnp.sin(ang)[None]
    x1r = kn[:, :B_ROPE_HALF]
    x2r = kn[:, B_ROPE_HALF:2 * B_ROPE_HALF]
    kfull = jnp.concatenate(
        [x1r * c - x2r * s, x2r * c + x1r * s, kn[:, 2 * B_ROPE_HALF:],
         jnp.zeros((B_KV_HEADS, B_K_PAD - B_HD, tm), F32)], axis=1)
    k_ref[0] = kfull.reshape(B_KV_HEADS * B_K_PAD, tm).T.astype(BF16)
    vt_ref[0] = _dot_nt(wvt_ref[...], hk).astype(BF16)


def _swa_in_kernel(x1_ref, g_ref, wqt_ref, wg_ref, pos_ref, inv_ref, gq_ref,
                   qt_ref, gate_ref, *, q_scale):
    tm = x1_ref.shape[0]
    hb = _rms_rows(x1_ref[...], g_ref[...]).astype(BF16)
    ang = inv_ref[...] * pos_ref[0]
    c = jnp.cos(ang)[None]
    s = jnp.sin(ang)[None]
    gq = gq_ref[...][None]
    hpc = 8
    for ch in range(B_HEADS // hpc):
        rows = slice(ch * hpc * B_HD, (ch + 1) * hpc * B_HD)
        qt = _dot_nt(wqt_ref[rows, :], hb).reshape(hpc, B_HD, tm)
        ss = jnp.sum(qt * qt, axis=1, keepdims=True)
        qn = qt * (lax.rsqrt(ss * (1.0 / B_HD) + EPS) * q_scale) * gq
        x1r = qn[:, :B_ROPE_HALF]
        x2r = qn[:, B_ROPE_HALF:2 * B_ROPE_HALF]
        qfull = jnp.concatenate(
            [x1r * c - x2r * s, x2r * c + x1r * s, qn[:, 2 * B_ROPE_HALF:]], axis=1)
        qt_ref[0, rows, :] = qfull.reshape(hpc * B_HD, tm).astype(BF16)
    n_chunk = 512
    for cc in range(gate_ref.shape[1] // n_chunk):
        sl = slice(cc * n_chunk, (cc + 1) * n_chunk)
        gate_ref[:, sl] = _silu(_dot(hb, wg_ref[:, sl])).astype(BF16)


def _swa_attn_kernel(sink_ref, qt_ref, kc_ref, kp_ref, vc_ref, vp_ref, gate_ref, o_ref, ot_ref):
    w = B_WINDOW
    n_blk = qt_ref.shape[2] // w
    j = pl.program_id(1)
    row = lax.broadcasted_iota(jnp.int32, (2 * w, B_GROUP * w), 0)
    col = lax.broadcasted_iota(jnp.int32, (2 * w, B_GROUP * w), 1)
    kj = row - w
    qi = col & (w - 1)
    band = (kj <= qi) & (kj > qi - w)
    head_in_group = col // w
    for blk in range(n_blk):
        qs = slice(blk * w, (blk + 1) * w)
        if blk == 0:
            valid = band & ((j > 0) | (kj >= 0))
        else:
            valid = band
        for g in range(B_KV_HEADS):
            ksl = slice(g * B_K_PAD, g * B_K_PAD + B_HD)
            vsl = slice(g * B_HD, (g + 1) * B_HD)
            if blk == 0:
                k_prev = kp_ref[0, :, ksl]
                v_prev = vp_ref[0, vsl, :]
            else:
                k_prev = kc_ref[0, (blk - 1) * w:blk * w, ksl]
                v_prev = vc_ref[0, vsl, (blk - 1) * w:blk * w]
            k_band = jnp.concatenate([k_prev, kc_ref[0, qs, ksl]], axis=0)
            v_band = jnp.concatenate([v_prev, vc_ref[0, vsl, qs]], axis=1)
            qt_g = jnp.concatenate(
                [qt_ref[0, (B_GROUP * g + i) * B_HD:(B_GROUP * g + i + 1) * B_HD, qs]
                 for i in range(B_GROUP)], axis=1)
            st = _dot(k_band, qt_g)
            st = jnp.where(valid, st, NEG)
            sink = jnp.full((1, B_GROUP * w), sink_ref[B_GROUP * g] * LOG2E, F32)
            for i in range(1, B_GROUP):
                sink = jnp.where(head_in_group[:1] == i, sink_ref[B_GROUP * g + i] * LOG2E, sink)
            m = jnp.maximum(jnp.max(st, axis=0, keepdims=True), sink)
            e = jnp.exp2(st - m)
            den = jnp.sum(e, axis=0, keepdims=True) + jnp.exp2(sink - m)
            ot = _dot(v_band, e.astype(BF16)) * (1.0 / den)
            for i in range(B_GROUP):
                hh = B_GROUP * g + i
                ot_ref[hh * B_HD:(hh + 1) * B_HD, :] = ot[:, i * w:(i + 1) * w]
        o_ref[qs, :] = (ot_ref[...].T * gate_ref[qs, :].astype(F32)).astype(BF16)


def _swa_out_kernel(og_ref, x1_ref, wo_ref, o_ref):
    o_ref[...] = x1_ref[...] + _dot(og_ref[...], wo_ref[...])


def kernel(x, positions, a_norm, a_w_in, a_q_latent_norm, a_w_q_up, a_kv_latent_norm, a_w_kv_up,
           a_q_head_norm, a_k_head_norm, a_w_o, kv_norm, kv_w, kv_k_head_norm, b_norm, b_w_in,
           b_q_head_norm, b_sinks, b_w_o):
    bsz, seq, d = x.shape
    t = bsz * seq
    assert a_norm.shape[0] == 1 and b_norm.shape[0] == 1
    assert seq % TQ_A == 0 and seq % QB_SWA == 0 and TQ_A % TK_A == 0

    xf = x.reshape(t, d)
    pos_col = positions.reshape(t, 1).astype(F32)
    pos_row = positions.reshape(bsz, 1, seq).astype(F32)

    w_in = a_w_in[0]
    n_lat = 2 * A_LORA + A_ROPE
    wl = jnp.pad(w_in[:, :n_lat], ((0, 0), (0, 128 - A_ROPE))).astype(BF16)
    wg = w_in[:, n_lat:].astype(BF16)
    wq = jnp.pad(a_w_q_up[0].reshape(A_LORA, A_HEADS, A_QK),
                 ((0, 0), (0, 0), (0, A_QK_PAD - A_QK))).reshape(A_LORA, A_HEADS * A_QK_PAD).astype(BF16)
    wkv = a_w_kv_up[0].reshape(A_LORA, A_HEADS, A_NOPE + A_V)
    wk = wkv[:, :, :A_NOPE].reshape(A_LORA, A_HEADS * A_NOPE).astype(BF16)
    wvt = wkv[:, :, A_NOPE:].reshape(A_LORA, A_HEADS * A_V).T.astype(BF16)
    gq_head = jnp.pad(a_q_head_norm[0], (0, A_QK_PAD - A_QK)).reshape(1, A_QK_PAD)
    gkn = a_k_head_norm[0][:A_NOPE].reshape(1, A_NOPE)
    gkr = jnp.pad(a_k_head_norm[0][A_NOPE:], (0, 128 - A_ROPE)).reshape(1, 128)
    inv_a = jnp.power(jnp.float32(ROPE_THETA), -jnp.arange(0, A_ROPE, 2, dtype=F32) / A_ROPE)
    inv_a_row = jnp.concatenate([inv_a, inv_a, jnp.zeros((64,), F32)]).reshape(1, 128)
    sign_a_row = jnp.concatenate(
        [-jnp.ones((32,), F32), jnp.ones((32,), F32), jnp.zeros((64,), F32)]).reshape(1, 128)
    wo_a = a_w_o[0].astype(BF16)

    kv_dim = B_KV_HEADS * B_HD
    wkt_b = kv_w[:, :kv_dim].T.astype(BF16)
    wvt_b = kv_w[:, kv_dim:].T.astype(BF16)
    rot_b = 2 * B_ROPE_HALF
    inv_b = jnp.power(jnp.float32(ROPE_THETA), -jnp.arange(0, rot_b, 2, dtype=F32) / rot_b)
    q_dim = B_HEADS * B_HD
    wqt_b = b_w_in[0][:, :q_dim].T.astype(BF16)
    wg_b = b_w_in[0][:, q_dim:].astype(BF16)
    wo_b = b_w_o[0].astype(BF16)

    n_in = t // TM_IN
    cq, ckv, kr, gate_a = pl.pallas_call(
        _mla_in_kernel,
        grid=(n_in,),
        in_specs=[
            pl.BlockSpec((TM_IN, d), lambda i: (i, 0)),
            _const_spec((1, d)),
            _const_spec(wl.shape),
            _const_spec(wg.shape),
            _const_spec((1, A_LORA)),
            _const_spec((1, A_LORA)),
        ],
        out_specs=[
            pl.BlockSpec((TM_IN, A_LORA), lambda i: (i, 0)),
            pl.BlockSpec((TM_IN, A_LORA), lambda i: (i, 0)),
            pl.BlockSpec((TM_IN, 128), lambda i: (i, 0)),
            pl.BlockSpec((TM_IN, d), lambda i: (i, 0)),
        ],
        out_shape=[
            jax.ShapeDtypeStruct((t, A_LORA), BF16),
            jax.ShapeDtypeStruct((t, A_LORA), BF16),
            jax.ShapeDtypeStruct((t, 128), F32),
            jax.ShapeDtypeStruct((t, A_HEADS * A_V), BF16),
        ],
        compiler_params=_params(1),
        name="mla_in",
    )(xf, a_norm[0].reshape(1, d), wl, wg, a_q_latent_norm[0].reshape(1, A_LORA),
      a_kv_latent_norm[0].reshape(1, A_LORA))

    ns_up = seq // TM_UP
    tok_up = lambda b, j: (b * ns_up + j, 0)
    q_a, k_a, vt_a = pl.pallas_call(
        functools.partial(_mla_up_kernel, q_scale=A_QK ** -0.5 * LOG2E),
        grid=(bsz, ns_up),
        in_specs=[
            pl.BlockSpec((TM_UP, A_LORA), tok_up),
            pl.BlockSpec((TM_UP, A_LORA), tok_up),
            pl.BlockSpec((TM_UP, 128), tok_up),
            pl.BlockSpec((TM_UP, 1), tok_up),
            _const_spec((1, 128)),
            _const_spec((1, 128)),
            _const_spec(wq.shape),
            _const_spec(wk.shape),
            _const_spec(wvt.shape),
            _const_spec((1, A_QK_PAD)),
            _const_spec((1, A_NOPE)),
            _const_spec((1, 128)),
        ],
        out_specs=[
            pl.BlockSpec((1, A_HEADS, TM_UP, A_QK_PAD), lambda b, j: (b, 0, j, 0)),
            pl.BlockSpec((1, A_HEADS, TM_UP, A_QK_PAD), lambda b, j: (b, 0, j, 0)),
            pl.BlockSpec((1, A_HEADS, A_V, TM_UP), lambda b, j: (b, 0, 0, j)),
        ],
        out_shape=[
            jax.ShapeDtypeStruct((bsz, A_HEADS, seq, A_QK_PAD), BF16),
            jax.ShapeDtypeStruct((bsz, A_HEADS, seq, A_QK_PAD), BF16),
            jax.ShapeDtypeStruct((bsz, A_HEADS, A_V, seq), BF16),
        ],
        compiler_params=_params(2),
        name="mla_up",
    )(cq, ckv, kr, pos_col, inv_a_row, sign_a_row, wq, wk, wvt, gq_head, gkn, gkr)

    og_a = pl.pallas_call(
        functools.partial(_mla_attn_kernel, tq=TQ_A, tk=TK_A),
        grid=(bsz, A_HEADS),
        in_specs=[
            pl.BlockSpec((1, 1, seq, A_QK_PAD), lambda b, h: (b, h, 0, 0)),
            pl.BlockSpec((1, 1, seq, A_QK_PAD), lambda b, h: (b, h, 0, 0)),
            pl.BlockSpec((1, 1, A_V, seq), lambda b, h: (b, h, 0, 0)),
            pl.BlockSpec((seq, A_V), lambda b, h: (b, h)),
        ],
        out_specs=pl.BlockSpec((seq, A_V), lambda b, h: (b, h)),
        out_shape=jax.ShapeDtypeStruct((t, A_HEADS * A_V), BF16),
        compiler_params=_params(2),
        name="mla_attn",
    )(q_a, k_a, vt_a, gate_a)

    ns_out = seq // TM_OUT
    tok_out = lambda b, j: (b * ns_out + j, 0)
    x1, k_b, vt_b = pl.pallas_call(
        _mla_out_kernel,
        grid=(bsz, ns_out),
        in_specs=[
            pl.BlockSpec((TM_OUT, d), tok_out),
            pl.BlockSpec((TM_OUT, d), tok_out),
            _const_spec(wo_a.shape),
            _const_spec((1, d)),
            _const_spec(wkt_b.shape),
            _const_spec(wvt_b.shape),
            pl.BlockSpec((1, 1, TM_OUT), lambda b, j: (b, 0, j)),
            _const_spec((B_ROPE_HALF, TM_OUT)),
            _const_spec((B_HD, TM_OUT)),
        ],
        out_specs=[
            pl.BlockSpec((TM_OUT, d), tok_out),
            pl.BlockSpec((1, TM_OUT, B_KV_HEADS * B_K_PAD), lambda b, j: (b, j, 0)),
            pl.BlockSpec((1, kv_dim, TM_OUT), lambda b, j: (b, 0, j)),
        ],
        out_shape=[
            jax.ShapeDtypeStruct((t, d), F32),
            jax.ShapeDtypeStruct((bsz, seq, B_KV_HEADS * B_K_PAD), BF16),
            jax.ShapeDtypeStruct((bsz, kv_dim, seq), BF16),
        ],
        compiler_params=_params(2),
        name="mla_out",
    )(og_a, xf, wo_a, kv_norm.reshape(1, d), wkt_b, wvt_b, pos_row,
      jnp.broadcast_to(inv_b[:, None], (B_ROPE_HALF, TM_OUT)),
      jnp.broadcast_to(kv_k_head_norm[:, None], (B_HD, TM_OUT)))

    ns_bin = seq // TM_BIN
    tok_bin = lambda b, j: (b * ns_bin + j, 0)
    qt_b, gate_b = pl.pallas_call(
        functools.partial(_swa_in_kernel, q_scale=B_HD ** -0.5 * LOG2E),
        grid=(bsz, ns_bin),
        in_specs=[
            pl.BlockSpec((TM_BIN, d), tok_bin),
            _const_spec((1, d)),
            _const_spec(wqt_b.shape),
            _const_spec(wg_b.shape),
            pl.BlockSpec((1, 1, TM_BIN), lambda b, j: (b, 0, j)),
            _const_spec((B_ROPE_HALF, TM_BIN)),
            _const_spec((B_HD, TM_BIN)),
        ],
        out_specs=[
            pl.BlockSpec((1, q_dim, TM_BIN), lambda b, j: (b, 0, j)),
            pl.BlockSpec((TM_BIN, q_dim), tok_bin),
        ],
        out_shape=[
            jax.ShapeDtypeStruct((bsz, q_dim, seq), BF16),
            jax.ShapeDtypeStruct((t, q_dim), BF16),
        ],
        compiler_params=_params(2),
        name="swa_in",
    )(x1, b_norm[0].reshape(1, d), wqt_b, wg_b, pos_row,
      jnp.broadcast_to(inv_b[:, None], (B_ROPE_HALF, TM_BIN)),
      jnp.broadcast_to(b_q_head_norm[0][:, None], (B_HD, TM_BIN)))

    ns_sw = seq // QB_SWA
    blk_per_step = QB_SWA // B_WINDOW
    prev_blk = lambda b, j: jnp.maximum(j * blk_per_step - 1, 0)
    og_b = pl.pallas_call(
        _swa_attn_kernel,
        grid=(bsz, ns_sw),
        in_specs=[
            pl.BlockSpec(memory_space=pltpu.SMEM),
            pl.BlockSpec((1, q_dim, QB_SWA), lambda b, j: (b, 0, j)),
            pl.BlockSpec((1, QB_SWA, B_KV_HEADS * B_K_PAD), lambda b, j: (b, j, 0)),
            pl.BlockSpec((1, B_WINDOW, B_KV_HEADS * B_K_PAD), lambda b, j: (b, prev_blk(b, j), 0)),
            pl.BlockSpec((1, kv_dim, QB_SWA), lambda b, j: (b, 0, j)),
            pl.BlockSpec((1, kv_dim, B_WINDOW), lambda b, j: (b, 0, prev_blk(b, j))),
            pl.BlockSpec((QB_SWA, q_dim), lambda b, j: (b * ns_sw + j, 0)),
        ],
        out_specs=pl.BlockSpec((QB_SWA, q_dim), lambda b, j: (b * ns_sw + j, 0)),
        out_shape=jax.ShapeDtypeStruct((t, q_dim), BF16),
        scratch_shapes=[pltpu.VMEM((q_dim, B_WINDOW), F32)],
        compiler_params=_params(2),
        name="swa_attn",
    )(b_sinks[0], qt_b, k_b, k_b, vt_b, vt_b, gate_b)

    n_out = t // TM_OUT
    out = pl.pallas_call(
        _swa_out_kernel,
        grid=(n_out,),
        in_specs=[
            pl.BlockSpec((TM_OUT, q_dim), lambda i: (i, 0)),
            pl.BlockSpec((TM_OUT, d), lambda i: (i, 0)),
            _const_spec(wo_b.shape),
        ],
        out_specs=pl.BlockSpec((TM_OUT, d), lambda i: (i, 0)),
        out_shape=jax.ShapeDtypeStruct((t, d), F32),
        compiler_params=_params(1),
        name="swa_out",
    )(og_b, x1, wo_b)
    return out.reshape(bsz, seq, d)
```

```python
import functools
import math

import jax
import jax.numpy as jnp
from jax import lax
from jax.experimental import pallas as pl
from jax.experimental.pallas import tpu as pltpu

F32 = jnp.float32
BF16 = jnp.bfloat16

EPS = 1e-6
ROPE_THETA = 500000.0
NEG = -1e30
LOG2E = math.log2(math.e)

A_HEADS = 16
A_LORA = 512
A_NOPE = 128
A_ROPE = 64
A_QK = A_NOPE + A_ROPE
A_QK_PAD = 256
A_V = 128

B_HEADS = 32
B_KV_HEADS = 8
B_GROUP = B_HEADS // B_KV_HEADS
B_HD = 64
B_ROPE_HALF = 8
B_WINDOW = 128
B_K_PAD = 128

VMEM_LIMIT = 56 * 1024 * 1024

TM_IN = 512
TM_UP = 256
TQ_A = 1024
TK_A = 512
TC_A = 256
TM_OUT = 256
TM_BIN = 256
QB_SWA = 512


def _const_spec(shape):
    return pl.BlockSpec(shape, lambda *_: (0,) * len(shape))


def _params(n_axes):
    return pltpu.CompilerParams(
        dimension_semantics=("arbitrary",) * n_axes, vmem_limit_bytes=VMEM_LIMIT)


def _rms_rows(xf, g_row):
    ms = jnp.mean(xf * xf, axis=-1, keepdims=True)
    return xf * lax.rsqrt(ms + EPS) * g_row


def _dot(a, b):
    return jnp.dot(a, b, preferred_element_type=F32)


def _dot_nt(a, b):
    return lax.dot_general(a, b, (((1,), (1,)), ((), ())), preferred_element_type=F32)


def _silu(g):
    return g * jax.nn.sigmoid(g)


def _mla_in_kernel(x_ref, g_ref, wl_ref, wg_ref, gq_ref, gkv_ref,
                   cq_ref, ckv_ref, kr_ref, gate_ref):
    h = _rms_rows(x_ref[...], g_ref[...]).astype(BF16)
    lat = _dot(h, wl_ref[...])
    cq_ref[...] = _rms_rows(lat[:, :A_LORA], gq_ref[...]).astype(BF16)
    ckv_ref[...] = _rms_rows(lat[:, A_LORA:2 * A_LORA], gkv_ref[...]).astype(BF16)
    kr_ref[...] = lat[:, 2 * A_LORA:]
    n_chunk = 512
    for c in range(gate_ref.shape[1] // n_chunk):
        sl = slice(c * n_chunk, (c + 1) * n_chunk)
        gate_ref[:, sl] = _silu(_dot(h, wg_ref[:, sl])).astype(BF16)


def _swap_half_64(t, lane):
    return jnp.where(lane < 32, pltpu.roll(t, 96, axis=1), pltpu.roll(t, 32, axis=1))


def _mla_up_kernel(cq_ref, ckv_ref, kr_ref, pos_ref, inv_ref, sign_ref,
                   wq_ref, wk_ref, wvt_ref, gq_ref, gkn_ref, gkr_ref,
                   q_ref, k_ref, vt_ref, *, q_scale):
    tm = cq_ref.shape[0]
    cq = cq_ref[...]
    ckv = ckv_ref[...]
    ang = pos_ref[...] * inv_ref[...]
    cos_t = jnp.cos(ang)
    sin_t = jnp.sin(ang) * sign_ref[...]
    lane = lax.broadcasted_iota(jnp.int32, (tm, 128), 1)

    def rope(t):
        return t * cos_t + _swap_half_64(t, lane) * sin_t

    gq = gq_ref[...]
    for hh in range(A_HEADS):
        qh = _dot(cq, wq_ref[:, hh * A_QK_PAD:(hh + 1) * A_QK_PAD])
        ss = jnp.sum(qh * qh, axis=-1, keepdims=True)
        r = lax.rsqrt(ss * (1.0 / A_QK) + EPS) * q_scale
        qn = qh * gq
        q_ref[0, hh, :, :A_NOPE] = (qn[:, :A_NOPE] * r).astype(BF16)
        q_ref[0, hh, :, A_NOPE:] = (rope(qn[:, A_NOPE:]) * r).astype(BF16)

    kr = kr_ref[...]
    ss_r = jnp.sum(kr * kr, axis=-1, keepdims=True)
    kr_rot = rope(kr * gkr_ref[...])
    gkn = gkn_ref[...]
    for hh in range(A_HEADS):
        kn = _dot(ckv, wk_ref[:, hh * A_NOPE:(hh + 1) * A_NOPE])
        ss = jnp.sum(kn * kn, axis=-1, keepdims=True) + ss_r
        r = lax.rsqrt(ss * (1.0 / A_QK) + EPS)
        k_ref[0, hh, :, :A_NOPE] = (kn * gkn * r).astype(BF16)
        k_ref[0, hh, :, A_NOPE:] = (kr_rot * r).astype(BF16)

    for c in range(4):
        rows = slice(c * 512, (c + 1) * 512)
        vt = _dot_nt(wvt_ref[rows, :], ckv)
        vt_ref[0, c * 4:(c + 1) * 4] = vt.reshape(4, A_V, tm).astype(BF16)


def _mla_attn_kernel(q_ref, k_ref, vt_ref, gate_ref, o_ref,
                     s0_ref, s1_ref, acc_ref, m_ref, l_ref, *, tq, tk, tc):
    seq = q_ref.shape[2]
    assert tq == 2 * tk
    n_c = tq // tc

    def q_body(qi, _):
        qs = pl.multiple_of(qi * tq, tq)

        def scores(s_ref, ks, c):
            cs = slice(c * tc, (c + 1) * tc)
            s_ref[:, cs] = _dot_nt(k_ref[0, 0, pl.ds(ks, tk), :],
                                   q_ref[0, 0, pl.ds(qs + c * tc, tc), :])

        def softmax_pv(s_ref, ks, c, mask):
            cs = slice(c * tc, (c + 1) * tc)
            st = s_ref[:, cs]
            if mask is not None:
                st = jnp.where(mask, st, NEG)
            m = m_ref[:, cs]
            m_new = jnp.maximum(m, jnp.max(st, axis=0, keepdims=True))
            p = jnp.exp2(st - m_new)
            alpha = jnp.exp2(m - m_new)
            m_ref[:, cs] = m_new
            l_ref[:, cs] = alpha * l_ref[:, cs] + jnp.sum(p, axis=0, keepdims=True)
            vt = vt_ref[0, 0, :, pl.ds(ks, tk)]
            acc_ref[:, cs] = alpha * acc_ref[:, cs] + _dot(vt, p.astype(BF16))

        def pair(j, _):
            ks = pl.multiple_of(j * tq, tq)
            for c in range(n_c):
                scores(s1_ref, ks + tk, c)
                softmax_pv(s0_ref, ks, c, None)
            for c in range(n_c):
                scores(s0_ref, ks + tq, c)
                softmax_pv(s1_ref, ks + tk, c, None)
            return 0

        acc_ref[...] = jnp.zeros_like(acc_ref)
        m_ref[...] = jnp.full_like(m_ref, NEG)
        l_ref[...] = jnp.zeros_like(l_ref)
        for c in range(n_c):
            scores(s0_ref, 0, c)
        lax.fori_loop(0, qi, pair, 0)
        row = lax.broadcasted_iota(jnp.int32, (tk, tc), 0)
        col = lax.broadcasted_iota(jnp.int32, (tk, tc), 1)
        for c in range(n_c):
            if (c + 1) * tc > tk:
                scores(s1_ref, qs + tk, c)
        for c in range(n_c):
            softmax_pv(s0_ref, qs, c, None if c * tc >= tk else row <= col + c * tc)
        for c in range(n_c):
            if (c + 1) * tc > tk:
                softmax_pv(s1_ref, qs + tk, c, row + tk <= col + c * tc)
        o = (acc_ref[...] * (1.0 / l_ref[...])).T
        o_ref[pl.ds(qs, tq), :] = (o * gate_ref[pl.ds(qs, tq), :].astype(F32)).astype(BF16)
        return 0

    lax.fori_loop(0, seq // tq, q_body, 0)


def _mla_out_kernel(og_ref, x_ref, wo_ref, gkv_ref, wkt_ref, wvt_ref, pos_ref, inv_ref, gk_ref,
                    x1_ref, k_ref, vt_ref):
    tm = x_ref.shape[0]
    x1 = x_ref[...] + _dot(og_ref[...], wo_ref[...])
    x1_ref[...] = x1
    hk = _rms_rows(x1, gkv_ref[...]).astype(BF16)
    kt = _dot_nt(wkt_ref[...], hk).reshape(B_KV_HEADS, B_HD, tm)
    ss = jnp.sum(kt * kt, axis=1, keepdims=True)
    kn = kt * lax.rsqrt(ss * (1.0 / B_HD) + EPS) * gk_ref[...][None]
    ang = inv_ref[...] * pos_ref[0]
    c = jnp.cos(ang)[None]
    s = jnp.sin(ang)[None]
    x1r = kn[:, :B_ROPE_HALF]
    x2r = kn[:, B_ROPE_HALF:2 * B_ROPE_HALF]
    kfull = jnp.concatenate(
        [x1r * c - x2r * s, x2r * c + x1r * s, kn[:, 2 * B_ROPE_HALF:],
         jnp.zeros((B_KV_HEADS, B_K_PAD - B_HD, tm), F32)], axis=1)
    k_ref[0] = kfull.reshape(B_KV_HEADS * B_K_PAD, tm).T.astype(BF16)
    vt_ref[0] = _dot_nt(wvt_ref[...], hk).astype(BF16)


def _swa_in_kernel(x1_ref, g_ref, wqt_ref, wg_ref, pos_ref, inv_ref, gq_ref,
                   qt_ref, gate_ref, *, q_scale):
    tm = x1_ref.shape[0]
    hb = _rms_rows(x1_ref[...], g_ref[...]).astype(BF16)
    ang = inv_ref[...] * pos_ref[0]
    c = jnp.cos(ang)[None]
    s = jnp.sin(ang)[None]
    gq = gq_ref[...][None]
    hpc = 8
    for ch in range(B_HEADS // hpc):
        rows = slice(ch * hpc * B_HD, (ch + 1) * hpc * B_HD)
        qt = _dot_nt(wqt_ref[rows, :], hb).reshape(hpc, B_HD, tm)
        ss = jnp.sum(qt * qt, axis=1, keepdims=True)
        qn = qt * (lax.rsqrt(ss * (1.0 / B_HD) + EPS) * q_scale) * gq
        x1r = qn[:, :B_ROPE_HALF]
        x2r = qn[:, B_ROPE_HALF:2 * B_ROPE_HALF]
        qfull = jnp.concatenate(
            [x1r * c - x2r * s, x2r * c + x1r * s, qn[:, 2 * B_ROPE_HALF:]], axis=1)
        qt_ref[0, rows, :] = qfull.reshape(hpc * B_HD, tm).astype(BF16)
    n_chunk = 512
    for cc in range(gate_ref.shape[1] // n_chunk):
        sl = slice(cc * n_chunk, (cc + 1) * n_chunk)
        gate_ref[:, sl] = _silu(_dot(hb, wg_ref[:, sl])).astype(BF16)


def _swa_attn_kernel(sink_ref, qt_ref, kc_ref, kp_ref, vc_ref, vp_ref, gate_ref, o_ref, ot_ref):
    w = B_WINDOW
    n_blk = qt_ref.shape[2] // w
    j = pl.program_id(1)
    row = lax.broadcasted_iota(jnp.int32, (2 * w, B_GROUP * w), 0)
    col = lax.broadcasted_iota(jnp.int32, (2 * w, B_GROUP * w), 1)
    kj = row - w
    qi = col & (w - 1)
    band = (kj <= qi) & (kj > qi - w)
    head_in_group = col // w
    for blk in range(n_blk):
        qs = slice(blk * w, (blk + 1) * w)
        if blk == 0:
            valid = band & ((j > 0) | (kj >= 0))
        else:
            valid = band
        for g in range(B_KV_HEADS):
            ksl = slice(g * B_K_PAD, g * B_K_PAD + B_HD)
            vsl = slice(g * B_HD, (g + 1) * B_HD)
            if blk == 0:
                k_prev = kp_ref[0, :, ksl]
                v_prev = vp_ref[0, vsl, :]
            else:
                k_prev = kc_ref[0, (blk - 1) * w:blk * w, ksl]
                v_prev = vc_ref[0, vsl, (blk - 1) * w:blk * w]
            k_band = jnp.concatenate([k_prev, kc_ref[0, qs, ksl]], axis=0)
            v_band = jnp.concatenate([v_prev, vc_ref[0, vsl, qs]], axis=1)
            qt_g = jnp.concatenate(
                [qt_ref[0, (B_GROUP * g + i) * B_HD:(B_GROUP * g + i + 1) * B_HD, qs]
                 for i in range(B_GROUP)], axis=1)
            st = _dot(k_band, qt_g)
            st = jnp.where(valid, st, NEG)
            sink = jnp.full((1, B_GROUP * w), sink_ref[B_GROUP * g] * LOG2E, F32)
            for i in range(1, B_GROUP):
                sink = jnp.where(head_in_group[:1] == i, sink_ref[B_GROUP * g + i] * LOG2E, sink)
            m = jnp.maximum(jnp.max(st, axis=0, keepdims=True), sink)
            e = jnp.exp2(st - m)
            den = jnp.sum(e, axis=0, keepdims=True) + jnp.exp2(sink - m)
            ot = _dot(v_band, e.astype(BF16)) * (1.0 / den)
            for i in range(B_GROUP):
                hh = B_GROUP * g + i
                ot_ref[hh * B_HD:(hh + 1) * B_HD, :] = ot[:, i * w:(i + 1) * w]
        o_ref[qs, :] = (ot_ref[...].T * gate_ref[qs, :].astype(F32)).astype(BF16)


def _swa_out_kernel(og_ref, x1_ref, wo_ref, o_ref):
    o_ref[...] = x1_ref[...] + _dot(og_ref[...], wo_ref[...])


def kernel(x, positions, a_norm, a_w_in, a_q_latent_norm, a_w_q_up, a_kv_latent_norm, a_w_kv_up,
           a_q_head_norm, a_k_head_norm, a_w_o, kv_norm, kv_w, kv_k_head_norm, b_norm, b_w_in,
           b_q_head_norm, b_sinks, b_w_o):
    bsz, seq, d = x.shape
    t = bsz * seq
    assert a_norm.shape[0] == 1 and b_norm.shape[0] == 1
    assert seq % TQ_A == 0 and seq % QB_SWA == 0 and TQ_A % TK_A == 0

    xf = x.reshape(t, d)
    pos_col = positions.reshape(t, 1).astype(F32)
    pos_row = positions.reshape(bsz, 1, seq).astype(F32)

    w_in = a_w_in[0]
    n_lat = 2 * A_LORA + A_ROPE
    wl = jnp.pad(w_in[:, :n_lat], ((0, 0), (0, 128 - A_ROPE))).astype(BF16)
    wg = w_in[:, n_lat:].astype(BF16)
    wq = jnp.pad(a_w_q_up[0].reshape(A_LORA, A_HEADS, A_QK),
                 ((0, 0), (0, 0), (0, A_QK_PAD - A_QK))).reshape(A_LORA, A_HEADS * A_QK_PAD).astype(BF16)
    wkv = a_w_kv_up[0].reshape(A_LORA, A_HEADS, A_NOPE + A_V)
    wk = wkv[:, :, :A_NOPE].reshape(A_LORA, A_HEADS * A_NOPE).astype(BF16)
    wvt = wkv[:, :, A_NOPE:].reshape(A_LORA, A_HEADS * A_V).T.astype(BF16)
    gq_head = jnp.pad(a_q_head_norm[0], (0, A_QK_PAD - A_QK)).reshape(1, A_QK_PAD)
    gkn = a_k_head_norm[0][:A_NOPE].reshape(1, A_NOPE)
    gkr = jnp.pad(a_k_head_norm[0][A_NOPE:], (0, 128 - A_ROPE)).reshape(1, 128)
    inv_a = jnp.power(jnp.float32(ROPE_THETA), -jnp.arange(0, A_ROPE, 2, dtype=F32) / A_ROPE)
    inv_a_row = jnp.concatenate([inv_a, inv_a, jnp.zeros((64,), F32)]).reshape(1, 128)
    sign_a_row = jnp.concatenate(
        [-jnp.ones((32,), F32), jnp.ones((32,), F32), jnp.zeros((64,), F32)]).reshape(1, 128)
    wo_a = a_w_o[0].astype(BF16)

    kv_dim = B_KV_HEADS * B_HD
    wkt_b = kv_w[:, :kv_dim].T.astype(BF16)
    wvt_b = kv_w[:, kv_dim:].T.astype(BF16)
    rot_b = 2 * B_ROPE_HALF
    inv_b = jnp.power(jnp.float32(ROPE_THETA), -jnp.arange(0, rot_b, 2, dtype=F32) / rot_b)
    q_dim = B_HEADS * B_HD
    wqt_b = b_w_in[0][:, :q_dim].T.astype(BF16)
    wg_b = b_w_in[0][:, q_dim:].astype(BF16)
    wo_b = b_w_o[0].astype(BF16)

    n_in = t // TM_IN
    cq, ckv, kr, gate_a = pl.pallas_call(
        _mla_in_kernel,
        grid=(n_in,),
        in_specs=[
            pl.BlockSpec((TM_IN, d), lambda i: (i, 0)),
            _const_spec((1, d)),
            _const_spec(wl.shape),
            _const_spec(wg.shape),
            _const_spec((1, A_LORA)),
            _const_spec((1, A_LORA)),
        ],
        out_specs=[
            pl.BlockSpec((TM_IN, A_LORA), lambda i: (i, 0)),
            pl.BlockSpec((TM_IN, A_LORA), lambda i: (i, 0)),
            pl.BlockSpec((TM_IN, 128), lambda i: (i, 0)),
            pl.BlockSpec((TM_IN, d), lambda i: (i, 0)),
        ],
        out_shape=[
            jax.ShapeDtypeStruct((t, A_LORA), BF16),
            jax.ShapeDtypeStruct((t, A_LORA), BF16),
            jax.ShapeDtypeStruct((t, 128), F32),
            jax.ShapeDtypeStruct((t, A_HEADS * A_V), BF16),
        ],
        compiler_params=_params(1),
        name="mla_in",
    )(xf, a_norm[0].reshape(1, d), wl, wg, a_q_latent_norm[0].reshape(1, A_LORA),
      a_kv_latent_norm[0].reshape(1, A_LORA))

    ns_up = seq // TM_UP
    tok_up = lambda b, j: (b * ns_up + j, 0)
    q_a, k_a, vt_a = pl.pallas_call(
        functools.partial(_mla_up_kernel, q_scale=A_QK ** -0.5 * LOG2E),
        grid=(bsz, ns_up),
        in_specs=[
            pl.BlockSpec((TM_UP, A_LORA), tok_up),
            pl.BlockSpec((TM_UP, A_LORA), tok_up),
            pl.BlockSpec((TM_UP, 128), tok_up),
            pl.BlockSpec((TM_UP, 1), tok_up),
            _const_spec((1, 128)),
            _const_spec((1, 128)),
            _const_spec(wq.shape),
            _const_spec(wk.shape),
            _const_spec(wvt.shape),
            _const_spec((1, A_QK_PAD)),
            _const_spec((1, A_NOPE)),
            _const_spec((1, 128)),
        ],
        out_specs=[
            pl.BlockSpec((1, A_HEADS, TM_UP, A_QK_PAD), lambda b, j: (b, 0, j, 0)),
            pl.BlockSpec((1, A_HEADS, TM_UP, A_QK_PAD), lambda b, j: (b, 0, j, 0)),
            pl.BlockSpec((1, A_HEADS, A_V, TM_UP), lambda b, j: (b, 0, 0, j)),
        ],
        out_shape=[
            jax.ShapeDtypeStruct((bsz, A_HEADS, seq, A_QK_PAD), BF16),
            jax.ShapeDtypeStruct((bsz, A_HEADS, seq, A_QK_PAD), BF16),
            jax.ShapeDtypeStruct((bsz, A_HEADS, A_V, seq), BF16),
        ],
        compiler_params=_params(2),
        name="mla_up",
    )(cq, ckv, kr, pos_col, inv_a_row, sign_a_row, wq, wk, wvt, gq_head, gkn, gkr)

    og_a = pl.pallas_call(
        functools.partial(_mla_attn_kernel, tq=TQ_A, tk=TK_A, tc=TC_A),
        grid=(bsz, A_HEADS),
        in_specs=[
            pl.BlockSpec((1, 1, seq, A_QK_PAD), lambda b, h: (b, h, 0, 0)),
            pl.BlockSpec((1, 1, seq, A_QK_PAD), lambda b, h: (b, h, 0, 0)),
            pl.BlockSpec((1, 1, A_V, seq), lambda b, h: (b, h, 0, 0)),
            pl.BlockSpec((seq, A_V), lambda b, h: (b, h)),
        ],
        out_specs=pl.BlockSpec((seq, A_V), lambda b, h: (b, h)),
        out_shape=jax.ShapeDtypeStruct((t, A_HEADS * A_V), BF16),
        scratch_shapes=[pltpu.VMEM((TK_A, TQ_A), F32), pltpu.VMEM((TK_A, TQ_A), F32),
                        pltpu.VMEM((A_V, TQ_A), F32), pltpu.VMEM((1, TQ_A), F32),
                        pltpu.VMEM((1, TQ_A), F32)],
        compiler_params=_params(2),
        name="mla_attn",
    )(q_a, k_a, vt_a, gate_a)

    ns_out = seq // TM_OUT
    tok_out = lambda b, j: (b * ns_out + j, 0)
    x1, k_b, vt_b = pl.pallas_call(
        _mla_out_kernel,
        grid=(bsz, ns_out),
        in_specs=[
            pl.BlockSpec((TM_OUT, d), tok_out),
            pl.BlockSpec((TM_OUT, d), tok_out),
            _const_spec(wo_a.shape),
            _const_spec((1, d)),
            _const_spec(wkt_b.shape),
            _const_spec(wvt_b.shape),
            pl.BlockSpec((1, 1, TM_OUT), lambda b, j: (b, 0, j)),
            _const_spec((B_ROPE_HALF, TM_OUT)),
            _const_spec((B_HD, TM_OUT)),
        ],
        out_specs=[
            pl.BlockSpec((TM_OUT, d), tok_out),
            pl.BlockSpec((1, TM_OUT, B_KV_HEADS * B_K_PAD), lambda b, j: (b, j, 0)),
            pl.BlockSpec((1, kv_dim, TM_OUT), lambda b, j: (b, 0, j)),
        ],
        out_shape=[
            jax.ShapeDtypeStruct((t, d), F32),
            jax.ShapeDtypeStruct((bsz, seq, B_KV_HEADS * B_K_PAD), BF16),
            jax.ShapeDtypeStruct((bsz, kv_dim, seq), BF16),
        ],
        compiler_params=_params(2),
        name="mla_out",
    )(og_a, xf, wo_a, kv_norm.reshape(1, d), wkt_b, wvt_b, pos_row,
      jnp.broadcast_to(inv_b[:, None], (B_ROPE_HALF, TM_OUT)),
      jnp.broadcast_to(kv_k_head_norm[:, None], (B_HD, TM_OUT)))

    ns_bin = seq // TM_BIN
    tok_bin = lambda b, j: (b * ns_bin + j, 0)
    qt_b, gate_b = pl.pallas_call(
        functools.partial(_swa_in_kernel, q_scale=B_HD ** -0.5 * LOG2E),
        grid=(bsz, ns_bin),
        in_specs=[
            pl.BlockSpec((TM_BIN, d), tok_bin),
            _const_spec((1, d)),
            _const_spec(wqt_b.shape),
            _const_spec(wg_b.shape),
            pl.BlockSpec((1, 1, TM_BIN), lambda b, j: (b, 0, j)),
            _const_spec((B_ROPE_HALF, TM_BIN)),
            _const_spec((B_HD, TM_BIN)),
        ],
        out_specs=[
            pl.BlockSpec((1, q_dim, TM_BIN), lambda b, j: (b, 0, j)),
            pl.BlockSpec((TM_BIN, q_dim), tok_bin),
        ],
        out_shape=[
            jax.ShapeDtypeStruct((bsz, q_dim, seq), BF16),
            jax.ShapeDtypeStruct((t, q_dim), BF16),
        ],
        compiler_params=_params(2),
        name="swa_in",
    )(x1, b_norm[0].reshape(1, d), wqt_b, wg_b, pos_row,
      jnp.broadcast_to(inv_b[:, None], (B_ROPE_HALF, TM_BIN)),
      jnp.broadcast_to(b_q_head_norm[0][:, None], (B_HD, TM_BIN)))

    ns_sw = seq // QB_SWA
    blk_per_step = QB_SWA // B_WINDOW
    prev_blk = lambda b, j: jnp.maximum(j * blk_per_step - 1, 0)
    og_b = pl.pallas_call(
        _swa_attn_kernel,
        grid=(bsz, ns_sw),
        in_specs=[
            pl.BlockSpec(memory_space=pltpu.SMEM),
            pl.BlockSpec((1, q_dim, QB_SWA), lambda b, j: (b, 0, j)),
            pl.BlockSpec((1, QB_SWA, B_KV_HEADS * B_K_PAD), lambda b, j: (b, j, 0)),
            pl.BlockSpec((1, B_WINDOW, B_KV_HEADS * B_K_PAD), lambda b, j: (b, prev_blk(b, j), 0)),
            pl.BlockSpec((1, kv_dim, QB_SWA), lambda b, j: (b, 0, j)),
            pl.BlockSpec((1, kv_dim, B_WINDOW), lambda b, j: (b, 0, prev_blk(b, j))),
            pl.BlockSpec((QB_SWA, q_dim), lambda b, j: (b * ns_sw + j, 0)),
        ],
        out_specs=pl.BlockSpec((QB_SWA, q_dim), lambda b, j: (b * ns_sw + j, 0)),
        out_shape=jax.ShapeDtypeStruct((t, q_dim), BF16),
        scratch_shapes=[pltpu.VMEM((q_dim, B_WINDOW), F32)],
        compiler_params=_params(2),
        name="swa_attn",
    )(b_sinks[0], qt_b, k_b, k_b, vt_b, vt_b, gate_b)

    n_out = t // TM_OUT
    out = pl.pallas_call(
        _swa_out_kernel,
        grid=(n_out,),
        in_specs=[
            pl.BlockSpec((TM_OUT, q_dim), lambda i: (i, 0)),
            pl.BlockSpec((TM_OUT, d), lambda i: (i, 0)),
            _const_spec(wo_b.shape),
        ],
        out_specs=pl.BlockSpec((TM_OUT, d), lambda i: (i, 0)),
        out_shape=jax.ShapeDtypeStruct((t, d), F32),
        compiler_params=_params(1),
        name="swa_out",
    )(og_b, x1, wo_b)
    return out.reshape(bsz, seq, d)
```

```python
import functools
import math

import jax
import jax.numpy as jnp
from jax import lax
from jax.experimental import pallas as pl
from jax.experimental.pallas import tpu as pltpu

F32 = jnp.float32
BF16 = jnp.bfloat16

EPS = 1e-6
ROPE_THETA = 500000.0
NEG = -1e30
LOG2E = math.log2(math.e)

A_HEADS = 16
A_LORA = 512
A_NOPE = 128
A_ROPE = 64
A_QK = A_NOPE + A_ROPE
A_QK_PAD = 256
A_V = 128
A_V_PAD = A_V + 16

B_HEADS = 32
B_KV_HEADS = 8
B_GROUP = B_HEADS // B_KV_HEADS
B_HD = 64
B_ROPE_HALF = 8
B_WINDOW = 128
B_K_PAD = 128

VMEM_LIMIT = 56 * 1024 * 1024

TM_IN = 512
TM_UP = 256
TQ_A = 1024
TK_A = 512
TC_A = 256
TM_OUT = 256
TM_BIN = 256
QB_SWA = 512


def _const_spec(shape):
    return pl.BlockSpec(shape, lambda *_: (0,) * len(shape))


def _params(n_axes):
    return pltpu.CompilerParams(
        dimension_semantics=("arbitrary",) * n_axes, vmem_limit_bytes=VMEM_LIMIT)


def _rms_rows(xf, g_row):
    ms = jnp.mean(xf * xf, axis=-1, keepdims=True)
    return xf * lax.rsqrt(ms + EPS) * g_row


def _dot(a, b):
    return jnp.dot(a, b, preferred_element_type=F32)


def _dot_nt(a, b):
    return lax.dot_general(a, b, (((1,), (1,)), ((), ())), preferred_element_type=F32)


def _silu(g):
    return g * jax.nn.sigmoid(g)


def _mla_in_kernel(x_ref, g_ref, wl_ref, wg_ref, gq_ref, gkv_ref,
                   cq_ref, ckv_ref, kr_ref, gate_ref):
    h = _rms_rows(x_ref[...], g_ref[...]).astype(BF16)
    lat = _dot(h, wl_ref[...])
    cq_ref[...] = _rms_rows(lat[:, :A_LORA], gq_ref[...]).astype(BF16)
    ckv_ref[...] = _rms_rows(lat[:, A_LORA:2 * A_LORA], gkv_ref[...]).astype(BF16)
    kr_ref[...] = lat[:, 2 * A_LORA:]
    n_chunk = 512
    for c in range(gate_ref.shape[1] // n_chunk):
        sl = slice(c * n_chunk, (c + 1) * n_chunk)
        gate_ref[:, sl] = _silu(_dot(h, wg_ref[:, sl])).astype(BF16)


def _swap_half_64(t, lane):
    return jnp.where(lane < 32, pltpu.roll(t, 96, axis=1), pltpu.roll(t, 32, axis=1))


def _mla_up_kernel(cq_ref, ckv_ref, kr_ref, posc_ref, posr_ref, inv_ref, sign_ref, invc_ref,
                   wqt_ref, wk_ref, wvt_ref, gqc_ref, gkn_ref, gkr_ref,
                   qt_ref, k_ref, vt_ref, *, q_scale):
    tm = cq_ref.shape[0]
    cq = cq_ref[...]
    ckv = ckv_ref[...]

    ang_t = invc_ref[...] * posr_ref[0]
    cos_c = jnp.cos(ang_t)[None]
    sin_c = jnp.sin(ang_t)[None]
    gqc = gqc_ref[...][None]
    hpc = 4
    half = A_ROPE // 2
    for c in range(A_HEADS // hpc):
        rows = slice(c * hpc * A_QK, (c + 1) * hpc * A_QK)
        qt = _dot_nt(wqt_ref[rows, :], cq).reshape(hpc, A_QK, tm)
        ss = jnp.sum(qt * qt, axis=1, keepdims=True)
        qn = qt * (lax.rsqrt(ss * (1.0 / A_QK) + EPS) * q_scale) * gqc
        x1 = qn[:, A_NOPE:A_NOPE + half]
        x2 = qn[:, A_NOPE + half:]
        full = jnp.concatenate(
            [qn[:, :A_NOPE], x1 * cos_c - x2 * sin_c, x2 * cos_c + x1 * sin_c,
             jnp.zeros((hpc, A_QK_PAD - A_QK, tm), F32)], axis=1)
        qt_ref[0, c * hpc:(c + 1) * hpc, 0] = full.astype(BF16)

    ang = posc_ref[...] * inv_ref[...]
    cos_t = jnp.cos(ang)
    sin_t = jnp.sin(ang) * sign_ref[...]
    lane = lax.broadcasted_iota(jnp.int32, (tm, 128), 1)
    kr = kr_ref[...]
    ss_r = jnp.sum(kr * kr, axis=-1, keepdims=True)
    krg = kr * gkr_ref[...]
    kr_rot = krg * cos_t + _swap_half_64(krg, lane) * sin_t
    gkn = gkn_ref[...]
    kn_all = _dot(ckv, wk_ref[...])
    for hh in range(A_HEADS):
        kn = kn_all[:, hh * A_NOPE:(hh + 1) * A_NOPE]
        ss = jnp.sum(kn * kn, axis=-1, keepdims=True) + ss_r
        r = lax.rsqrt(ss * (1.0 / A_QK) + EPS)
        k_ref[0, hh, :, :A_NOPE] = (kn * gkn * r).astype(BF16)
        k_ref[0, hh, :, A_NOPE:] = (kr_rot * r).astype(BF16)

    for c in range(A_HEADS // hpc):
        rows = slice(c * hpc * A_V, (c + 1) * hpc * A_V)
        vt = _dot_nt(wvt_ref[rows, :], ckv)
        vt_ref[0, c * hpc:(c + 1) * hpc, 0, :A_V, :] = vt.reshape(hpc, A_V, tm).astype(BF16)
    ones_row = lax.broadcasted_iota(jnp.int32, (A_HEADS, A_V_PAD - A_V, tm), 1) == 0
    vt_ref[0, :, 0, A_V:, :] = jnp.where(ones_row, 1.0, 0.0).astype(BF16)


def _mla_attn_kernel(qt_ref, k_ref, vt_ref, gate_ref, o_ref,
                     s0_ref, s1_ref, acc_ref, m_ref, *, tq, tk, tc):
    seq = k_ref.shape[2]
    tv = vt_ref.shape[4]
    assert tq == 2 * tk and qt_ref.shape[4] == tc and tk % tv == 0
    n_c = tq // tc

    def scores(s_ref, ks, qs, c):
        cs = slice(c * tc, (c + 1) * tc)
        s_ref[:, cs] = _dot(k_ref[0, 0, pl.ds(ks, tk), :], qt_ref[0, 0, qs // tc + c])

    def softmax_pv(s_ref, ks, c, mask):
        cs = slice(c * tc, (c + 1) * tc)
        st = s_ref[:, cs]
        if mask is not None:
            st = jnp.where(mask, st, NEG)
        m = m_ref[:, cs]
        m_new = jnp.maximum(m, jnp.max(st, axis=0, keepdims=True))
        p = jnp.exp2(st - m_new).astype(BF16)
        alpha = jnp.exp2(m - m_new)
        m_ref[:, cs] = m_new
        vt = jnp.concatenate([vt_ref[0, 0, ks // tv + i] for i in range(tk // tv)], axis=1)
        acc_ref[:, cs] = alpha * acc_ref[:, cs] + _dot(vt, p)

    def q_body(qi, _):
        qs = pl.multiple_of(qi * tq, tq)

        def pair(j, _):
            ks = pl.multiple_of(j * tq, tq)
            for c in range(n_c):
                scores(s1_ref, ks + tk, qs, c)
                softmax_pv(s0_ref, ks, c, None)
            for c in range(n_c):
                scores(s0_ref, ks + tq, qs, c)
                softmax_pv(s1_ref, ks + tk, c, None)
            return 0

        def finish(c):
            cs = slice(c * tc, (c + 1) * tc)
            o = (acc_ref[:A_V, cs] * (1.0 / acc_ref[A_V:A_V + 1, cs])).T
            rows = pl.ds(qs + c * tc, tc)
            o_ref[rows, :] = (o * gate_ref[rows, :].astype(F32)).astype(BF16)

        acc_ref[...] = jnp.zeros_like(acc_ref)
        m_ref[...] = jnp.full_like(m_ref, NEG)
        odd = qi & 1

        @pl.when(odd == 1)
        def _():
            pair(0, 0)

        def two_pairs(j2, _):
            pair(odd + 2 * j2, 0)
            pair(odd + 2 * j2 + 1, 0)
            return 0

        lax.fori_loop(0, lax.shift_right_logical(qi, 1), two_pairs, 0)
        row = lax.broadcasted_iota(jnp.int32, (tk, tc), 0)
        col = lax.broadcasted_iota(jnp.int32, (tk, tc), 1)
        for c in range(n_c):
            if (c + 1) * tc > tk:
                scores(s1_ref, qs + tk, qs, c)
        for c in range(n_c):
            softmax_pv(s0_ref, qs, c, None if c * tc >= tk else row <= col + c * tc)
            if (c + 1) * tc <= tk:
                finish(c)
        qs_next = pl.multiple_of(jnp.minimum(qs + tq, seq - tq), tq)
        for c in range(n_c):
            scores(s0_ref, 0, qs_next, c)
        for c in range(n_c):
            if (c + 1) * tc > tk:
                softmax_pv(s1_ref, qs + tk, c, row + tk <= col + c * tc)
                finish(c)
        return 0

    for c in range(n_c):
        scores(s0_ref, 0, 0, c)
    lax.fori_loop(0, seq // tq, q_body, 0)


def _mla_out_kernel(og_ref, x_ref, wo_ref, gkv_ref, wkt_ref, wvt_ref, pos_ref, inv_ref, gk_ref,
                    x1_ref, k_ref, vt_ref):
    tm = x_ref.shape[0]
    x1 = x_ref[...] + _dot(og_ref[...], wo_ref[...])
    x1_ref[...] = x1
    hk = _rms_rows(x1, gkv_ref[...]).astype(BF16)
    kt = _dot_nt(wkt_ref[...], hk).reshape(B_KV_HEADS, B_HD, tm)
    ss = jnp.sum(kt * kt, axis=1, keepdims=True)
    kn = kt * lax.rsqrt(ss * (1.0 / B_HD) + EPS) * gk_ref[...][None]
    ang = inv_ref[...] * pos_ref[0]
    c = jnp.cos(ang)[None]
    s = jnp.sin(ang)[None]
    x1r = kn[:, :B_ROPE_HALF]
    x2r = kn[:, B_ROPE_HALF:2 * B_ROPE_HALF]
    kfull = jnp.concatenate(
        [x1r * c - x2r * s, x2r * c + x1r * s, kn[:, 2 * B_ROPE_HALF:],
         jnp.zeros((B_KV_HEADS, B_K_PAD - B_HD, tm), F32)], axis=1)
    k_ref[0] = kfull.reshape(B_KV_HEADS * B_K_PAD, tm).T.astype(BF16)
    vt_ref[0] = _dot_nt(wvt_ref[...], hk).astype(BF16)


def _swa_in_kernel(x1_ref, g_ref, wqt_ref, wg_ref, pos_ref, inv_ref, gq_ref,
                   qt_ref, gate_ref, *, q_scale):
    tm = x1_ref.shape[0]
    hb = _rms_rows(x1_ref[...], g_ref[...]).astype(BF16)
    ang = inv_ref[...] * pos_ref[0]
    c = jnp.cos(ang)[None]
    s = jnp.sin(ang)[None]
    gq = gq_ref[...][None]
    hpc = 8
    for ch in range(B_HEADS // hpc):
        rows = slice(ch * hpc * B_HD, (ch + 1) * hpc * B_HD)
        qt = _dot_nt(wqt_ref[rows, :], hb).reshape(hpc, B_HD, tm)
        ss = jnp.sum(qt * qt, axis=1, keepdims=True)
        qn = qt * (lax.rsqrt(ss * (1.0 / B_HD) + EPS) * q_scale) * gq
        x1r = qn[:, :B_ROPE_HALF]
        x2r = qn[:, B_ROPE_HALF:2 * B_ROPE_HALF]
        qfull = jnp.concatenate(
            [x1r * c - x2r * s, x2r * c + x1r * s, qn[:, 2 * B_ROPE_HALF:]], axis=1)
        qt_ref[0, rows, :] = qfull.reshape(hpc * B_HD, tm).astype(BF16)
    n_chunk = 512
    for cc in range(gate_ref.shape[1] // n_chunk):
        sl = slice(cc * n_chunk, (cc + 1) * n_chunk)
        gate_ref[:, sl] = _silu(_dot(hb, wg_ref[:, sl])).astype(BF16)


def _swa_attn_kernel(sink_ref, qt_ref, kc_ref, kp_ref, vc_ref, vp_ref, gate_ref, o_ref, ot_ref):
    w = B_WINDOW
    n_blk = qt_ref.shape[2] // w
    j = pl.program_id(1)
    row = lax.broadcasted_iota(jnp.int32, (2 * w, B_GROUP * w), 0)
    col = lax.broadcasted_iota(jnp.int32, (2 * w, B_GROUP * w), 1)
    kj = row - w
    qi = col & (w - 1)
    band = (kj <= qi) & (kj > qi - w)
    head_in_group = col // w
    for blk in range(n_blk):
        qs = slice(blk * w, (blk + 1) * w)
        if blk == 0:
            valid = band & ((j > 0) | (kj >= 0))
        else:
            valid = band
        for g in range(B_KV_HEADS):
            ksl = slice(g * B_K_PAD, g * B_K_PAD + B_HD)
            vsl = slice(g * B_HD, (g + 1) * B_HD)
            if blk == 0:
                k_prev = kp_ref[0, :, ksl]
                v_prev = vp_ref[0, vsl, :]
            else:
                k_prev = kc_ref[0, (blk - 1) * w:blk * w, ksl]
                v_prev = vc_ref[0, vsl, (blk - 1) * w:blk * w]
            k_band = jnp.concatenate([k_prev, kc_ref[0, qs, ksl]], axis=0)
            v_band = jnp.concatenate([v_prev, vc_ref[0, vsl, qs]], axis=1)
            qt_g = jnp.concatenate(
                [qt_ref[0, (B_GROUP * g + i) * B_HD:(B_GROUP * g + i + 1) * B_HD, qs]
                 for i in range(B_GROUP)], axis=1)
            st = _dot(k_band, qt_g)
            st = jnp.where(valid, st, NEG)
            sink = jnp.full((1, B_GROUP * w), sink_ref[B_GROUP * g] * LOG2E, F32)
            for i in range(1, B_GROUP):
                sink = jnp.where(head_in_group[:1] == i, sink_ref[B_GROUP * g + i] * LOG2E, sink)
            m = jnp.maximum(jnp.max(st, axis=0, keepdims=True), sink)
            e = jnp.exp2(st - m)
            den = jnp.sum(e, axis=0, keepdims=True) + jnp.exp2(sink - m)
            ot = _dot(v_band, e.astype(BF16)) * (1.0 / den)
            for i in range(B_GROUP):
                hh = B_GROUP * g + i
                ot_ref[hh * B_HD:(hh + 1) * B_HD, :] = ot[:, i * w:(i + 1) * w]
        o_ref[qs, :] = (ot_ref[...].T * gate_ref[qs, :].astype(F32)).astype(BF16)


def _swa_out_kernel(og_ref, x1_ref, wo_ref, o_ref):
    o_ref[...] = x1_ref[...] + _dot(og_ref[...], wo_ref[...])


def kernel(x, positions, a_norm, a_w_in, a_q_latent_norm, a_w_q_up, a_kv_latent_norm, a_w_kv_up,
           a_q_head_norm, a_k_head_norm, a_w_o, kv_norm, kv_w, kv_k_head_norm, b_norm, b_w_in,
           b_q_head_norm, b_sinks, b_w_o):
    bsz, seq, d = x.shape
    t = bsz * seq
    assert a_norm.shape[0] == 1 and b_norm.shape[0] == 1
    assert seq % TQ_A == 0 and seq % QB_SWA == 0 and TQ_A % TK_A == 0

    xf = x.reshape(t, d)
    pos_col = positions.reshape(t, 1).astype(F32)
    pos_row = positions.reshape(bsz, 1, seq).astype(F32)

    w_in = a_w_in[0]
    n_lat = 2 * A_LORA + A_ROPE
    wl = jnp.pad(w_in[:, :n_lat], ((0, 0), (0, 128 - A_ROPE))).astype(BF16)
    wg = w_in[:, n_lat:].astype(BF16)
    wqt = a_w_q_up[0].T.astype(BF16)
    wkv = a_w_kv_up[0].reshape(A_LORA, A_HEADS, A_NOPE + A_V)
    wk = wkv[:, :, :A_NOPE].reshape(A_LORA, A_HEADS * A_NOPE).astype(BF16)
    wvt = wkv[:, :, A_NOPE:].reshape(A_LORA, A_HEADS * A_V).T.astype(BF16)
    gq_col = jnp.broadcast_to(a_q_head_norm[0][:, None], (A_QK, TM_UP))
    gkn =a_k_head_norm[0][:A_NOPE].reshape(1, A_NOPE)
    gkr = jnp.pad(a_k_head_norm[0][A_NOPE:], (0, 128 - A_ROPE)).reshape(1, 128)
    inv_a = jnp.power(jnp.float32(ROPE_THETA), -jnp.arange(0, A_ROPE, 2, dtype=F32) / A_ROPE)
    inv_a_row = jnp.concatenate([inv_a, inv_a, jnp.zeros((64,), F32)]).reshape(1, 128)
    inv_a_col = jnp.broadcast_to(inv_a[:, None], (A_ROPE // 2, TM_UP))
    sign_a_row = jnp.concatenate(
        [-jnp.ones((32,), F32), jnp.ones((32,), F32), jnp.zeros((64,), F32)]).reshape(1, 128)
    wo_a = a_w_o[0].astype(BF16)

    kv_dim = B_KV_HEADS * B_HD
    wkt_b = kv_w[:, :kv_dim].T.astype(BF16)
    wvt_b = kv_w[:, kv_dim:].T.astype(BF16)
    rot_b = 2 * B_ROPE_HALF
    inv_b = jnp.power(jnp.float32(ROPE_THETA), -jnp.arange(0, rot_b, 2, dtype=F32) / rot_b)
    q_dim = B_HEADS * B_HD
    wqt_b = b_w_in[0][:, :q_dim].T.astype(BF16)
    wg_b = b_w_in[0][:, q_dim:].astype(BF16)
    wo_b = b_w_o[0].astype(BF16)

    n_in = t // TM_IN
    cq, ckv, kr, gate_a = pl.pallas_call(
        _mla_in_kernel,
        grid=(n_in,),
        in_specs=[
            pl.BlockSpec((TM_IN, d), lambda i: (i, 0)),
            _const_spec((1, d)),
            _const_spec(wl.shape),
            _const_spec(wg.shape),
            _const_spec((1, A_LORA)),
            _const_spec((1, A_LORA)),
        ],
        out_specs=[
            pl.BlockSpec((TM_IN, A_LORA), lambda i: (i, 0)),
            pl.BlockSpec((TM_IN, A_LORA), lambda i: (i, 0)),
            pl.BlockSpec((TM_IN, 128), lambda i: (i, 0)),
            pl.BlockSpec((TM_IN, d), lambda i: (i, 0)),
        ],
        out_shape=[
            jax.ShapeDtypeStruct((t, A_LORA), BF16),
            jax.ShapeDtypeStruct((t, A_LORA), BF16),
            jax.ShapeDtypeStruct((t, 128), F32),
            jax.ShapeDtypeStruct((t, A_HEADS * A_V), BF16),
        ],
        compiler_params=_params(1),
        name="mla_in",
    )(xf, a_norm[0].reshape(1, d), wl, wg, a_q_latent_norm[0].reshape(1, A_LORA),
      a_kv_latent_norm[0].reshape(1, A_LORA))

    ns_up = seq // TM_UP
    tok_up = lambda b, j: (b * ns_up + j, 0)
    qt_a, k_a, vt_a = pl.pallas_call(
        functools.partial(_mla_up_kernel, q_scale=A_QK ** -0.5 * LOG2E),
        grid=(bsz, ns_up),
        in_specs=[
            pl.BlockSpec((TM_UP, A_LORA), tok_up),
            pl.BlockSpec((TM_UP, A_LORA), tok_up),
            pl.BlockSpec((TM_UP, 128), tok_up),
            pl.BlockSpec((TM_UP, 1), tok_up),
            pl.BlockSpec((1, 1, TM_UP), lambda b, j: (b, 0, j)),
            _const_spec((1, 128)),
            _const_spec((1, 128)),
            _const_spec(inv_a_col.shape),
            _const_spec(wqt.shape),
            _const_spec(wk.shape),
            _const_spec(wvt.shape),
            _const_spec(gq_col.shape),
            _const_spec((1, A_NOPE)),
            _const_spec((1, 128)),
        ],
        out_specs=[
            pl.BlockSpec((1, A_HEADS, 1, A_QK_PAD, TM_UP), lambda b, j: (b, 0, j, 0, 0)),
            pl.BlockSpec((1, A_HEADS, TM_UP, A_QK_PAD), lambda b, j: (b, 0, j, 0)),
            pl.BlockSpec((1, A_HEADS, 1, A_V_PAD, TM_UP), lambda b, j: (b, 0, j, 0, 0)),
        ],
        out_shape=[
            jax.ShapeDtypeStruct((bsz, A_HEADS, ns_up, A_QK_PAD, TM_UP), BF16),
            jax.ShapeDtypeStruct((bsz, A_HEADS, seq, A_QK_PAD), BF16),
            jax.ShapeDtypeStruct((bsz, A_HEADS, ns_up, A_V_PAD, TM_UP), BF16),
        ],
        compiler_params=_params(2),
        name="mla_up",
    )(cq, ckv, kr, pos_col, pos_row, inv_a_row, sign_a_row, inv_a_col, wqt, wk, wvt, gq_col, gkn, gkr)

    og_a = pl.pallas_call(
        functools.partial(_mla_attn_kernel, tq=TQ_A, tk=TK_A, tc=TC_A),
        grid=(bsz, A_HEADS),
        in_specs=[
            pl.BlockSpec((1, 1, ns_up, A_QK_PAD, TM_UP), lambda b, h: (b, h, 0, 0, 0)),
            pl.BlockSpec((1, 1, seq, A_QK_PAD), lambda b, h: (b, h, 0, 0)),
            pl.BlockSpec((1, 1, ns_up, A_V_PAD, TM_UP), lambda b, h: (b, h, 0, 0, 0)),
            pl.BlockSpec((seq, A_V), lambda b, h: (b, h)),
        ],
        out_specs=pl.BlockSpec((seq, A_V), lambda b, h: (b, h)),
        out_shape=jax.ShapeDtypeStruct((t, A_HEADS * A_V), BF16),
        scratch_shapes=[pltpu.VMEM((TK_A, TQ_A), F32), pltpu.VMEM((TK_A, TQ_A), F32),
                        pltpu.VMEM((A_V_PAD, TQ_A), F32), pltpu.VMEM((1, TQ_A), F32)],
        compiler_params=_params(2),
        name="mla_attn",
    )(qt_a, k_a, vt_a, gate_a)

    ns_out = seq // TM_OUT
    tok_out = lambda b, j: (b * ns_out + j, 0)
    x1, k_b, vt_b = pl.pallas_call(
        _mla_out_kernel,
        grid=(bsz, ns_out),
        in_specs=[
            pl.BlockSpec((TM_OUT, d), tok_out),
            pl.BlockSpec((TM_OUT, d), tok_out),
            _const_spec(wo_a.shape),
            _const_spec((1, d)),
            _const_spec(wkt_b.shape),
            _const_spec(wvt_b.shape),
            pl.BlockSpec((1, 1, TM_OUT), lambda b, j: (b, 0, j)),
            _const_spec((B_ROPE_HALF, TM_OUT)),
            _const_spec((B_HD, TM_OUT)),
        ],
        out_specs=[
            pl.BlockSpec((TM_OUT, d), tok_out),
            pl.BlockSpec((1, TM_OUT, B_KV_HEADS * B_K_PAD), lambda b, j: (b, j, 0)),
            pl.BlockSpec((1, kv_dim, TM_OUT), lambda b, j: (b, 0, j)),
        ],
        out_shape=[
            jax.ShapeDtypeStruct((t, d), F32),
            jax.ShapeDtypeStruct((bsz, seq, B_KV_HEADS * B_K_PAD), BF16),
            jax.ShapeDtypeStruct((bsz, kv_dim, seq), BF16),
        ],
        compiler_params=_params(2),
        name="mla_out",
    )(og_a, xf, wo_a, kv_norm.reshape(1, d), wkt_b, wvt_b, pos_row,
      jnp.broadcast_to(inv_b[:, None], (B_ROPE_HALF, TM_OUT)),
      jnp.broadcast_to(kv_k_head_norm[:, None], (B_HD, TM_OUT)))

    ns_bin = seq // TM_BIN
    tok_bin = lambda b, j: (b * ns_bin + j, 0)
    qt_b, gate_b = pl.pallas_call(
        functools.partial(_swa_in_kernel, q_scale=B_HD ** -0.5 * LOG2E),
        grid=(bsz, ns_bin),
        in_specs=[
            pl.BlockSpec((TM_BIN, d), tok_bin),
            _const_spec((1, d)),
            _const_spec(wqt_b.shape),
            _const_spec(wg_b.shape),
            pl.BlockSpec((1, 1, TM_BIN), lambda b, j: (b, 0, j)),
            _const_spec((B_ROPE_HALF, TM_BIN)),
            _const_spec((B_HD, TM_BIN)),
        ],
        out_specs=[
            pl.BlockSpec((1, q_dim, TM_BIN), lambda b, j: (b, 0, j)),
            pl.BlockSpec((TM_BIN, q_dim), tok_bin),
        ],
        out_shape=[
            jax.ShapeDtypeStruct((bsz, q_dim, seq), BF16),
            jax.ShapeDtypeStruct((t, q_dim), BF16),
        ],
        compiler_params=_params(2),
        name="swa_in",
    )(x1, b_norm[0].reshape(1, d), wqt_b, wg_b, pos_row,
      jnp.broadcast_to(inv_b[:, None], (B_ROPE_HALF, TM_BIN)),
      jnp.broadcast_to(b_q_head_norm[0][:, None], (B_HD, TM_BIN)))

    ns_sw = seq // QB_SWA
    blk_per_step = QB_SWA // B_WINDOW
    prev_blk = lambda b, j: jnp.maximum(j * blk_per_step - 1, 0)
    og_b = pl.pallas_call(
        _swa_attn_kernel,
        grid=(bsz, ns_sw),
        in_specs=[
            pl.BlockSpec(memory_space=pltpu.SMEM),
            pl.BlockSpec((1, q_dim, QB_SWA), lambda b, j: (b, 0, j)),
            pl.BlockSpec((1, QB_SWA, B_KV_HEADS * B_K_PAD), lambda b, j: (b, j, 0)),
            pl.BlockSpec((1, B_WINDOW, B_KV_HEADS * B_K_PAD), lambda b, j: (b, prev_blk(b, j), 0)),
            pl.BlockSpec((1, kv_dim, QB_SWA), lambda b, j: (b, 0, j)),
            pl.BlockSpec((1, kv_dim, B_WINDOW), lambda b, j: (b, 0, prev_blk(b, j))),
            pl.BlockSpec((QB_SWA, q_dim), lambda b, j: (b * ns_sw + j, 0)),
        ],
        out_specs=pl.BlockSpec((QB_SWA, q_dim), lambda b, j: (b * ns_sw + j, 0)),
        out_shape=jax.ShapeDtypeStruct((t, q_dim), BF16),
        scratch_shapes=[pltpu.VMEM((q_dim, B_WINDOW), F32)],
        compiler_params=_params(2),
        name="swa_attn",
    )(b_sinks[0], qt_b, k_b, k_b, vt_b, vt_b, gate_b)

    n_out = t // TM_OUT
    out = pl.pallas_call(
        _swa_out_kernel,
        grid=(n_out,),
        in_specs=[
            pl.BlockSpec((TM_OUT, q_dim), lambda i: (i, 0)),
            pl.BlockSpec((TM_OUT, d), lambda i: (i, 0)),
            _const_spec(wo_b.shape),
        ],
        out_specs=pl.BlockSpec((TM_OUT, d), lambda i: (i, 0)),
        out_shape=jax.ShapeDtypeStruct((t, d), F32),
        compiler_params=_params(1),
        name="swa_out",
    )(og_b, x1, wo_b)
    return out.reshape(bsz, seq, d)
```

```python
import functools
import math

import jax
import jax.numpy as jnp
from jax import lax
from jax.experimental import pallas as pl
from jax.experimental.pallas import tpu as pltpu

F32 = jnp.float32
BF16 = jnp.bfloat16

EPS = 1e-6
ROPE_THETA = 500000.0
NEG = -1e30
LOG2E = math.log2(math.e)

A_HEADS = 16
A_LORA = 512
A_NOPE = 128
A_ROPE = 64
A_QK = A_NOPE + A_ROPE
A_QK_PAD = 256
A_V = 128
A_V_PAD = A_V + 16

B_HEADS = 32
B_KV_HEADS = 8
B_GROUP = B_HEADS // B_KV_HEADS
B_HD = 64
B_ROPE_HALF = 8
B_WINDOW = 128
B_K_PAD = 128
B_V_PAD = B_HD + 16

VMEM_LIMIT = 56 * 1024 * 1024

TM_IN = 512
TM_UP = 256
TQ_A = 1024
TK_A = 512
TC_A = 256
TM_OUT = 256
TM_BIN = 256
QB_SWA = 512


def _const_spec(shape):
    return pl.BlockSpec(shape, lambda *_: (0,) * len(shape))


def _params(n_axes):
    return pltpu.CompilerParams(
        dimension_semantics=("arbitrary",) * n_axes, vmem_limit_bytes=VMEM_LIMIT)


def _rms_rows(xf, g_row):
    ms = jnp.mean(xf * xf, axis=-1, keepdims=True)
    return xf * lax.rsqrt(ms + EPS) * g_row


def _dot(a, b):
    return jnp.dot(a, b, preferred_element_type=F32)


def _dot_nt(a, b):
    return lax.dot_general(a, b, (((1,), (1,)), ((), ())), preferred_element_type=F32)


def _silu(g):
    return g * jax.nn.sigmoid(g)


def _mla_in_kernel(x_ref, g_ref, wl_ref, wg_ref, gq_ref, gkv_ref,
                   cq_ref, ckv_ref, kr_ref, gate_ref):
    h = _rms_rows(x_ref[...], g_ref[...]).astype(BF16)
    lat = _dot(h, wl_ref[...])
    cq_ref[...] = _rms_rows(lat[:, :A_LORA], gq_ref[...]).astype(BF16)
    ckv_ref[...] = _rms_rows(lat[:, A_LORA:2 * A_LORA], gkv_ref[...]).astype(BF16)
    kr_ref[...] = lat[:, 2 * A_LORA:]
    n_chunk = 512
    for c in range(gate_ref.shape[1] // n_chunk):
        sl = slice(c * n_chunk, (c + 1) * n_chunk)
        gate_ref[:, sl] = _silu(_dot(h, wg_ref[:, sl])).astype(BF16)


def _swap_half_64(t, lane):
    return jnp.where(lane < 32, pltpu.roll(t, 96, axis=1), pltpu.roll(t, 32, axis=1))


def _mla_up_kernel(cq_ref, ckv_ref, kr_ref, posc_ref, posr_ref, inv_ref, sign_ref, invc_ref,
                   wqt_ref, wk_ref, wvt_ref, gqc_ref, gkn_ref, gkr_ref,
                   qt_ref, k_ref, vt_ref, *, q_scale):
    tm = cq_ref.shape[0]
    cq = cq_ref[...]
    ckv = ckv_ref[...]

    ang_t = invc_ref[...] * posr_ref[0]
    cos_c = jnp.cos(ang_t)[None]
    sin_c = jnp.sin(ang_t)[None]
    gqc = gqc_ref[...][None]
    hpc = 4
    half = A_ROPE // 2
    for c in range(A_HEADS // hpc):
        rows = slice(c * hpc * A_QK, (c + 1) * hpc * A_QK)
        qt = _dot_nt(wqt_ref[rows, :], cq).reshape(hpc, A_QK, tm)
        ss = jnp.sum(qt * qt, axis=1, keepdims=True)
        qn = qt * (lax.rsqrt(ss * (1.0 / A_QK) + EPS) * q_scale) * gqc
        x1 = qn[:, A_NOPE:A_NOPE + half]
        x2 = qn[:, A_NOPE + half:]
        full = jnp.concatenate(
            [qn[:, :A_NOPE], x1 * cos_c - x2 * sin_c, x2 * cos_c + x1 * sin_c,
             jnp.zeros((hpc, A_QK_PAD - A_QK, tm), F32)], axis=1)
        qt_ref[0, c * hpc:(c + 1) * hpc, 0] = full.astype(BF16)

    ang = posc_ref[...] * inv_ref[...]
    cos_t = jnp.cos(ang)
    sin_t = jnp.sin(ang) * sign_ref[...]
    lane = lax.broadcasted_iota(jnp.int32, (tm, 128), 1)
    kr = kr_ref[...]
    ss_r = jnp.sum(kr * kr, axis=-1, keepdims=True)
    krg = kr * gkr_ref[...]
    kr_rot = krg * cos_t + _swap_half_64(krg, lane) * sin_t
    gkn = gkn_ref[...]
    kn_all = _dot(ckv, wk_ref[...])
    for hh in range(A_HEADS):
        kn = kn_all[:, hh * A_NOPE:(hh + 1) * A_NOPE]
        ss = jnp.sum(kn * kn, axis=-1, keepdims=True) + ss_r
        r = lax.rsqrt(ss * (1.0 / A_QK) + EPS)
        k_ref[0, hh, :, :A_NOPE] = (kn * gkn * r).astype(BF16)
        k_ref[0, hh, :, A_NOPE:] = (kr_rot * r).astype(BF16)

    for c in range(A_HEADS // hpc):
        rows = slice(c * hpc * A_V, (c + 1) * hpc * A_V)
        vt = _dot_nt(wvt_ref[rows, :], ckv)
        vt_ref[0, c * hpc:(c + 1) * hpc, 0, :A_V, :] = vt.reshape(hpc, A_V, tm).astype(BF16)
    ones_row = lax.broadcasted_iota(jnp.int32, (A_HEADS, A_V_PAD - A_V, tm), 1) == 0
    vt_ref[0, :, 0, A_V:, :] = jnp.where(ones_row, 1.0, 0.0).astype(BF16)


def _mla_attn_kernel(qt_ref, k_ref, vt_ref, gate_ref, o_ref,
                     s0_ref, s1_ref, acc_ref, m_ref, *, tq, tk, tc):
    seq = k_ref.shape[2]
    tv = vt_ref.shape[4]
    assert tq == 2 * tk and qt_ref.shape[4] == tc and tk % tv == 0
    n_c = tq // tc

    def scores(s_ref, ks, qs, c):
        cs = slice(c * tc, (c + 1) * tc)
        s_ref[:, cs] = _dot(k_ref[0, 0, pl.ds(ks, tk), :], qt_ref[0, 0, qs // tc + c])

    def softmax_pv(s_ref, ks, c, mask):
        cs = slice(c * tc, (c + 1) * tc)
        st = s_ref[:, cs]
        if mask is not None:
            st = jnp.where(mask, st, NEG)
        m = m_ref[:, cs]
        m_new = jnp.maximum(m, jnp.max(st, axis=0, keepdims=True))
        p = jnp.exp2(st - m_new).astype(BF16)
        alpha = jnp.exp2(m - m_new)
        m_ref[:, cs] = m_new
        vt = jnp.concatenate([vt_ref[0, 0, ks // tv + i] for i in range(tk // tv)], axis=1)
        acc_ref[:, cs] = alpha * acc_ref[:, cs] + _dot(vt, p)

    def q_body(qi, _):
        qs = pl.multiple_of(qi * tq, tq)

        def pair(j, _):
            ks = pl.multiple_of(j * tq, tq)
            for c in range(n_c):
                scores(s1_ref, ks + tk, qs, c)
                softmax_pv(s0_ref, ks, c, None)
            for c in range(n_c):
                scores(s0_ref, ks + tq, qs, c)
                softmax_pv(s1_ref, ks + tk, c, None)
            return 0

        def finish(c):
            cs = slice(c * tc, (c + 1) * tc)
            o = (acc_ref[:A_V, cs] * (1.0 / acc_ref[A_V:A_V + 1, cs])).T
            rows = pl.ds(qs + c * tc, tc)
            o_ref[rows, :] = (o * gate_ref[rows, :].astype(F32)).astype(BF16)

        acc_ref[...] = jnp.zeros_like(acc_ref)
        m_ref[...] = jnp.full_like(m_ref, NEG)
        odd = qi & 1

        @pl.when(odd == 1)
        def _():
            pair(0, 0)

        def two_pairs(j2, _):
            pair(odd + 2 * j2, 0)
            pair(odd + 2 * j2 + 1, 0)
            return 0

        lax.fori_loop(0, lax.shift_right_logical(qi, 1), two_pairs, 0)
        row = lax.broadcasted_iota(jnp.int32, (tk, tc), 0)
        col = lax.broadcasted_iota(jnp.int32, (tk, tc), 1)
        for c in range(n_c):
            if (c + 1) * tc > tk:
                scores(s1_ref, qs + tk, qs, c)
        for c in range(n_c):
            softmax_pv(s0_ref, qs, c, None if c * tc >= tk else row <= col + c * tc)
            if (c + 1) * tc <= tk:
                finish(c)
        qs_next = pl.multiple_of(jnp.minimum(qs + tq, seq - tq), tq)
        for c in range(n_c):
            scores(s0_ref, 0, qs_next, c)
        for c in range(n_c):
            if (c + 1) * tc > tk:
                softmax_pv(s1_ref, qs + tk, c, row + tk <= col + c * tc)
                finish(c)
        return 0

    for c in range(n_c):
        scores(s0_ref, 0, 0, c)
    lax.fori_loop(0, seq // tq, q_body, 0)


def _mla_out_kernel(og_ref, x_ref, wo_ref, gkv_ref, wkt_ref, wvt_ref, pos_ref, inv_ref, gk_ref,
                    x1_ref, k_ref, vt_ref):
    tm = x_ref.shape[0]
    x1 = x_ref[...] + _dot(og_ref[...], wo_ref[...])
    x1_ref[...] = x1
    hk = _rms_rows(x1, gkv_ref[...]).astype(BF16)
    kt = _dot_nt(wkt_ref[...], hk).reshape(B_KV_HEADS, B_HD, tm)
    ss = jnp.sum(kt * kt, axis=1, keepdims=True)
    kn = kt * lax.rsqrt(ss * (1.0 / B_HD) + EPS) * gk_ref[...][None]
    ang = inv_ref[...] * pos_ref[0]
    c = jnp.cos(ang)[None]
    s = jnp.sin(ang)[None]
    x1r = kn[:, :B_ROPE_HALF]
    x2r = kn[:, B_ROPE_HALF:2 * B_ROPE_HALF]
    kfull = jnp.concatenate(
        [x1r * c - x2r * s, x2r * c + x1r * s, kn[:, 2 * B_ROPE_HALF:],
         jnp.zeros((B_KV_HEADS, B_K_PAD - B_HD, tm), F32)], axis=1)
    k_ref[0] = kfull.reshape(B_KV_HEADS * B_K_PAD, tm).T.astype(BF16)
    vt = _dot_nt(wvt_ref[...], hk).reshape(B_KV_HEADS, B_HD, tm)
    vt_ref[0, :, :B_HD, :] = vt.astype(BF16)
    ones_row = lax.broadcasted_iota(jnp.int32, (B_KV_HEADS, B_V_PAD - B_HD, tm), 1) == 0
    vt_ref[0, :, B_HD:, :] = jnp.where(ones_row, 1.0, 0.0).astype(BF16)


def _swa_in_kernel(x1_ref, g_ref, wqt_ref, wg_ref, pos_ref, inv_ref, gq_ref,
                   qt_ref, gate_ref, *, q_scale):
    tm = x1_ref.shape[0]
    hb = _rms_rows(x1_ref[...], g_ref[...]).astype(BF16)
    ang = inv_ref[...] * pos_ref[0]
    c = jnp.cos(ang)[None]
    s = jnp.sin(ang)[None]
    gq = gq_ref[...][None]
    hpc = 8
    for ch in range(B_HEADS // hpc):
        rows = slice(ch * hpc * B_HD, (ch + 1) * hpc * B_HD)
        qt = _dot_nt(wqt_ref[rows, :], hb).reshape(hpc, B_HD, tm)
        ss = jnp.sum(qt * qt, axis=1, keepdims=True)
        qn = qt * (lax.rsqrt(ss * (1.0 / B_HD) + EPS) * q_scale) * gq
        x1r = qn[:, :B_ROPE_HALF]
        x2r = qn[:, B_ROPE_HALF:2 * B_ROPE_HALF]
        qfull = jnp.concatenate(
            [x1r * c - x2r * s, x2r * c + x1r * s, qn[:, 2 * B_ROPE_HALF:]], axis=1)
        qt_ref[0, rows, :] = qfull.reshape(hpc * B_HD, tm).astype(BF16)
    n_chunk = 512
    for cc in range(gate_ref.shape[1] // n_chunk):
        sl = slice(cc * n_chunk, (cc + 1) * n_chunk)
        gate_ref[:, sl] = _silu(_dot(hb, wg_ref[:, sl])).astype(BF16)


def _swa_attn_kernel(sink_ref, qt_ref, kc_ref, kp_ref, vc_ref, vp_ref, gate_ref, o_ref,
                     ot0_ref, ot1_ref):
    w = B_WINDOW
    n_blk = qt_ref.shape[2] // w
    j = pl.program_id(1)
    gw = B_GROUP * w
    row = lax.broadcasted_iota(jnp.int32, (w, gw), 0)
    col = lax.broadcasted_iota(jnp.int32, (w, gw), 1)
    cur_visible = row <= (col & (w - 1))
    head_in_group = lax.broadcasted_iota(jnp.int32, (1, gw), 1) // w
    has_prev = j > 0

    def band_scores(blk, g):
        qs = slice(blk * w, (blk + 1) * w)
        ksl = slice(g * B_K_PAD, g * B_K_PAD + B_HD)
        if blk == 0:
            k_prev = kp_ref[0, :, ksl]
        else:
            k_prev = kc_ref[0, (blk - 1) * w:blk * w, ksl]
        k_band = jnp.concatenate([k_prev, kc_ref[0, qs, ksl]], axis=0)
        qt_g = jnp.concatenate(
            [qt_ref[0, (B_GROUP * g + i) * B_HD:(B_GROUP * g + i + 1) * B_HD, qs]
             for i in range(B_GROUP)], axis=1)
        return _dot(k_band, qt_g)

    def softmax_pv(st, blk, g, ot_ref):
        qs = slice(blk * w, (blk + 1) * w)
        s_prev = st[:w]
        if blk == 0:
            s_prev = jnp.where(has_prev, s_prev, NEG)
        s = jnp.where(cur_visible, st[w:], s_prev)
        sink = jnp.full((1, gw), sink_ref[B_GROUP * g] * LOG2E, F32)
        for i in range(1, B_GROUP):
            sink = jnp.where(head_in_group == i, sink_ref[B_GROUP * g + i] * LOG2E, sink)
        m = jnp.maximum(jnp.max(s, axis=0, keepdims=True), sink)
        e = jnp.exp2(s - m)
        e_cur = jnp.where(cur_visible, e, 0.0)
        p = jnp.concatenate([e - e_cur, e_cur], axis=0).astype(BF16)
        if blk == 0:
            v_prev = vp_ref[0, g]
        else:
            v_prev = vc_ref[0, g, :, (blk - 1) * w:blk * w]
        v_band = jnp.concatenate([v_prev, vc_ref[0, g, :, qs]], axis=1)
        pv = _dot(v_band, p)
        den = pv[B_HD:B_HD + 1] + jnp.exp2(sink - m)
        ot = pv[:B_HD] * (1.0 / den)
        for i in range(B_GROUP):
            hh = B_GROUP * g + i
            ot_ref[hh * B_HD:(hh + 1) * B_HD, :] = ot[:, i * w:(i + 1) * w]

    order = [(blk, g) for blk in range(n_blk) for g in range(B_KV_HEADS)]
    st = band_scores(*order[0])
    for idx, (blk, g) in enumerate(order):
        st_next = band_scores(*order[idx + 1]) if idx + 1 < len(order) else None
        ot_ref = ot0_ref if blk % 2 == 0 else ot1_ref
        softmax_pv(st, blk, g, ot_ref)
        if g == B_KV_HEADS - 1:
            qs = slice(blk * w, (blk + 1) * w)
            o_ref[qs, :] = (ot_ref[...].T * gate_ref[qs, :].astype(F32)).astype(BF16)
        st = st_next


def _swa_out_kernel(og_ref, x1_ref, wo_ref, o_ref):
    o_ref[...] = x1_ref[...] + _dot(og_ref[...], wo_ref[...])


def kernel(x, positions, a_norm, a_w_in, a_q_latent_norm, a_w_q_up, a_kv_latent_norm, a_w_kv_up,
           a_q_head_norm, a_k_head_norm, a_w_o, kv_norm, kv_w, kv_k_head_norm, b_norm, b_w_in,
           b_q_head_norm, b_sinks, b_w_o):
    bsz, seq, d = x.shape
    t = bsz * seq
    assert a_norm.shape[0] == 1 and b_norm.shape[0] == 1
    assert seq % TQ_A == 0 and seq % QB_SWA == 0 and TQ_A % TK_A == 0

    xf = x.reshape(t, d)
    pos_col = positions.reshape(t, 1).astype(F32)
    pos_row = positions.reshape(bsz, 1, seq).astype(F32)

    w_in = a_w_in[0]
    n_lat = 2 * A_LORA + A_ROPE
    wl = jnp.pad(w_in[:, :n_lat], ((0, 0), (0, 128 - A_ROPE))).astype(BF16)
    wg = w_in[:, n_lat:].astype(BF16)
    wqt = a_w_q_up[0].T.astype(BF16)
    wkv = a_w_kv_up[0].reshape(A_LORA, A_HEADS, A_NOPE + A_V)
    wk = wkv[:, :, :A_NOPE].reshape(A_LORA, A_HEADS * A_NOPE).astype(BF16)
    wvt = wkv[:, :, A_NOPE:].reshape(A_LORA, A_HEADS * A_V).T.astype(BF16)
    gq_col = jnp.broadcast_to(a_q_head_norm[0][:, None], (A_QK, TM_UP))
    gkn =a_k_head_norm[0][:A_NOPE].reshape(1, A_NOPE)
    gkr = jnp.pad(a_k_head_norm[0][A_NOPE:], (0, 128 - A_ROPE)).reshape(1, 128)
    inv_a = jnp.power(jnp.float32(ROPE_THETA), -jnp.arange(0, A_ROPE, 2, dtype=F32) / A_ROPE)
    inv_a_row = jnp.concatenate([inv_a, inv_a, jnp.zeros((64,), F32)]).reshape(1, 128)
    inv_a_col = jnp.broadcast_to(inv_a[:, None], (A_ROPE // 2, TM_UP))
    sign_a_row = jnp.concatenate(
        [-jnp.ones((32,), F32), jnp.ones((32,), F32), jnp.zeros((64,), F32)]).reshape(1, 128)
    wo_a = a_w_o[0].astype(BF16)

    kv_dim = B_KV_HEADS * B_HD
    wkt_b = kv_w[:, :kv_dim].T.astype(BF16)
    wvt_b = kv_w[:, kv_dim:].T.astype(BF16)
    rot_b = 2 * B_ROPE_HALF
    inv_b = jnp.power(jnp.float32(ROPE_THETA), -jnp.arange(0, rot_b, 2, dtype=F32) / rot_b)
    q_dim = B_HEADS * B_HD
    wqt_b = b_w_in[0][:, :q_dim].T.astype(BF16)
    wg_b = b_w_in[0][:, q_dim:].astype(BF16)
    wo_b = b_w_o[0].astype(BF16)

    n_in = t // TM_IN
    cq, ckv, kr, gate_a = pl.pallas_call(
        _mla_in_kernel,
        grid=(n_in,),
        in_specs=[
            pl.BlockSpec((TM_IN, d), lambda i: (i, 0)),
            _const_spec((1, d)),
            _const_spec(wl.shape),
            _const_spec(wg.shape),
            _const_spec((1, A_LORA)),
            _const_spec((1, A_LORA)),
        ],
        out_specs=[
            pl.BlockSpec((TM_IN, A_LORA), lambda i: (i, 0)),
            pl.BlockSpec((TM_IN, A_LORA), lambda i: (i, 0)),
            pl.BlockSpec((TM_IN, 128), lambda i: (i, 0)),
            pl.BlockSpec((TM_IN, d), lambda i: (i, 0)),
        ],
        out_shape=[
            jax.ShapeDtypeStruct((t, A_LORA), BF16),
            jax.ShapeDtypeStruct((t, A_LORA), BF16),
            jax.ShapeDtypeStruct((t, 128), F32),
            jax.ShapeDtypeStruct((t, A_HEADS * A_V), BF16),
        ],
        compiler_params=_params(1),
        name="mla_in",
    )(xf, a_norm[0].reshape(1, d), wl, wg, a_q_latent_norm[0].reshape(1, A_LORA),
      a_kv_latent_norm[0].reshape(1, A_LORA))

    ns_up = seq // TM_UP
    tok_up = lambda b, j: (b * ns_up + j, 0)
    qt_a, k_a, vt_a = pl.pallas_call(
        functools.partial(_mla_up_kernel, q_scale=A_QK ** -0.5 * LOG2E),
        grid=(bsz, ns_up),
        in_specs=[
            pl.BlockSpec((TM_UP, A_LORA), tok_up),
            pl.BlockSpec((TM_UP, A_LORA), tok_up),
            pl.BlockSpec((TM_UP, 128), tok_up),
            pl.BlockSpec((TM_UP, 1), tok_up),
            pl.BlockSpec((1, 1, TM_UP), lambda b, j: (b, 0, j)),
            _const_spec((1, 128)),
            _const_spec((1, 128)),
            _const_spec(inv_a_col.shape),
            _const_spec(wqt.shape),
            _const_spec(wk.shape),
            _const_spec(wvt.shape),
            _const_spec(gq_col.shape),
            _const_spec((1, A_NOPE)),
            _const_spec((1, 128)),
        ],
        out_specs=[
            pl.BlockSpec((1, A_HEADS, 1, A_QK_PAD, TM_UP), lambda b, j: (b, 0, j, 0, 0)),
            pl.BlockSpec((1, A_HEADS, TM_UP, A_QK_PAD), lambda b, j: (b, 0, j, 0)),
            pl.BlockSpec((1, A_HEADS, 1, A_V_PAD, TM_UP), lambda b, j: (b, 0, j, 0, 0)),
        ],
        out_shape=[
            jax.ShapeDtypeStruct((bsz, A_HEADS, ns_up, A_QK_PAD, TM_UP), BF16),
            jax.ShapeDtypeStruct((bsz, A_HEADS, seq, A_QK_PAD), BF16),
            jax.ShapeDtypeStruct((bsz, A_HEADS, ns_up, A_V_PAD, TM_UP), BF16),
        ],
        compiler_params=_params(2),
        name="mla_up",
    )(cq, ckv, kr, pos_col, pos_row, inv_a_row, sign_a_row, inv_a_col, wqt, wk, wvt, gq_col, gkn, gkr)

    og_a = pl.pallas_call(
        functools.partial(_mla_attn_kernel, tq=TQ_A, tk=TK_A, tc=TC_A),
        grid=(bsz, A_HEADS),
        in_specs=[
            pl.BlockSpec((1, 1, ns_up, A_QK_PAD, TM_UP), lambda b, h: (b, h, 0, 0, 0)),
            pl.BlockSpec((1, 1, seq, A_QK_PAD), lambda b, h: (b, h, 0, 0)),
            pl.BlockSpec((1, 1, ns_up, A_V_PAD, TM_UP), lambda b, h: (b, h, 0, 0, 0)),
            pl.BlockSpec((seq, A_V), lambda b, h: (b, h)),
        ],
        out_specs=pl.BlockSpec((seq, A_V), lambda b, h: (b, h)),
        out_shape=jax.ShapeDtypeStruct((t, A_HEADS * A_V), BF16),
        scratch_shapes=[pltpu.VMEM((TK_A, TQ_A), F32), pltpu.VMEM((TK_A, TQ_A), F32),
                        pltpu.VMEM((A_V_PAD, TQ_A), F32), pltpu.VMEM((1, TQ_A), F32)],
        compiler_params=_params(2),
        name="mla_attn",
    )(qt_a, k_a, vt_a, gate_a)

    ns_out = seq // TM_OUT
    tok_out = lambda b, j: (b * ns_out + j, 0)
    x1, k_b, vt_b = pl.pallas_call(
        _mla_out_kernel,
        grid=(bsz, ns_out),
        in_specs=[
            pl.BlockSpec((TM_OUT, d), tok_out),
            pl.BlockSpec((TM_OUT, d), tok_out),
            _const_spec(wo_a.shape),
            _const_spec((1, d)),
            _const_spec(wkt_b.shape),
            _const_spec(wvt_b.shape),
            pl.BlockSpec((1, 1, TM_OUT), lambda b, j: (b, 0, j)),
            _const_spec((B_ROPE_HALF, TM_OUT)),
            _const_spec((B_HD, TM_OUT)),
        ],
        out_specs=[
            pl.BlockSpec((TM_OUT, d), tok_out),
            pl.BlockSpec((1, TM_OUT, B_KV_HEADS * B_K_PAD), lambda b, j: (b, j, 0)),
            pl.BlockSpec((1, B_KV_HEADS, B_V_PAD, TM_OUT), lambda b, j: (b, 0, 0, j)),
        ],
        out_shape=[
            jax.ShapeDtypeStruct((t, d), F32),
            jax.ShapeDtypeStruct((bsz, seq, B_KV_HEADS * B_K_PAD), BF16),
            jax.ShapeDtypeStruct((bsz, B_KV_HEADS, B_V_PAD, seq), BF16),
        ],
        compiler_params=_params(2),
        name="mla_out",
    )(og_a, xf, wo_a, kv_norm.reshape(1, d), wkt_b, wvt_b, pos_row,
      jnp.broadcast_to(inv_b[:, None], (B_ROPE_HALF, TM_OUT)),
      jnp.broadcast_to(kv_k_head_norm[:, None], (B_HD, TM_OUT)))

    ns_bin = seq // TM_BIN
    tok_bin = lambda b, j: (b * ns_bin + j, 0)
    qt_b, gate_b = pl.pallas_call(
        functools.partial(_swa_in_kernel, q_scale=B_HD ** -0.5 * LOG2E),
        grid=(bsz, ns_bin),
        in_specs=[
            pl.BlockSpec((TM_BIN, d), tok_bin),
            _const_spec((1, d)),
            _const_spec(wqt_b.shape),
            _const_spec(wg_b.shape),
            pl.BlockSpec((1, 1, TM_BIN), lambda b, j: (b, 0, j)),
            _const_spec((B_ROPE_HALF, TM_BIN)),
            _const_spec((B_HD, TM_BIN)),
        ],
        out_specs=[
            pl.BlockSpec((1, q_dim, TM_BIN), lambda b, j: (b, 0, j)),
            pl.BlockSpec((TM_BIN, q_dim), tok_bin),
        ],
        out_shape=[
            jax.ShapeDtypeStruct((bsz, q_dim, seq), BF16),
            jax.ShapeDtypeStruct((t, q_dim), BF16),
        ],
        compiler_params=_params(2),
        name="swa_in",
    )(x1, b_norm[0].reshape(1, d), wqt_b, wg_b, pos_row,
      jnp.broadcast_to(inv_b[:, None], (B_ROPE_HALF, TM_BIN)),
      jnp.broadcast_to(b_q_head_norm[0][:, None], (B_HD, TM_BIN)))

    ns_sw = seq // QB_SWA
    blk_per_step = QB_SWA // B_WINDOW
    prev_blk = lambda b, j: jnp.maximum(j * blk_per_step - 1, 0)
    og_b = pl.pallas_call(
        _swa_attn_kernel,
        grid=(bsz, ns_sw),
        in_specs=[
            pl.BlockSpec(memory_space=pltpu.SMEM),
            pl.BlockSpec((1, q_dim, QB_SWA), lambda b, j: (b, 0, j)),
            pl.BlockSpec((1, QB_SWA, B_KV_HEADS * B_K_PAD), lambda b, j: (b, j, 0)),
            pl.BlockSpec((1, B_WINDOW, B_KV_HEADS * B_K_PAD), lambda b, j: (b, prev_blk(b, j), 0)),
            pl.BlockSpec((1, B_KV_HEADS, B_V_PAD, QB_SWA), lambda b, j: (b, 0, 0, j)),
            pl.BlockSpec((1, B_KV_HEADS, B_V_PAD, B_WINDOW), lambda b, j: (b, 0, 0, prev_blk(b, j))),
            pl.BlockSpec((QB_SWA, q_dim), lambda b, j: (b * ns_sw + j, 0)),
        ],
        out_specs=pl.BlockSpec((QB_SWA, q_dim), lambda b, j: (b * ns_sw + j, 0)),
        out_shape=jax.ShapeDtypeStruct((t, q_dim), BF16),
        scratch_shapes=[pltpu.VMEM((q_dim, B_WINDOW), F32), pltpu.VMEM((q_dim, B_WINDOW), F32)],
        compiler_params=_params(2),
        name="swa_attn",
    )(b_sinks[0], qt_b, k_b, k_b, vt_b, vt_b, gate_b)

    n_out = t // TM_OUT
    out = pl.pallas_call(
        _swa_out_kernel,
        grid=(n_out,),
        in_specs=[
            pl.BlockSpec((TM_OUT, q_dim), lambda i: (i, 0)),
            pl.BlockSpec((TM_OUT, d), lambda i: (i, 0)),
            _const_spec(wo_b.shape),
        ],
        out_specs=pl.BlockSpec((TM_OUT, d), lambda i: (i, 0)),
        out_shape=jax.ShapeDtypeStruct((t, d), F32),
        compiler_params=_params(1),
        name="swa_out",
    )(og_b, x1, wo_b)
    return out.reshape(bsz, seq, d)
```

```python
import functools
import math

import jax
import jax.numpy as jnp
from jax import lax
from jax.experimental import pallas as pl
from jax.experimental.pallas import tpu as pltpu

F32 = jnp.float32
BF16 = jnp.bfloat16

EPS = 1e-6
ROPE_THETA = 500000.0
NEG = -1e30
LOG2E = math.log2(math.e)

A_HEADS = 16
A_LORA = 512
A_NOPE = 128
A_ROPE = 64
A_QK = A_NOPE + A_ROPE
A_QK_PAD = 256
A_V = 128
A_V_PAD = A_V + 16

B_HEADS = 32
B_KV_HEADS = 8
B_GROUP = B_HEADS // B_KV_HEADS
B_HD = 64
B_ROPE_HALF = 8
B_WINDOW = 128
B_K_PAD = 128
B_V_PAD = B_HD + 16

VMEM_LIMIT = 56 * 1024 * 1024

TM_IN = 512
TM_UP = 256
TQ_A = 1024
TK_A = 512
TC_A = 256
TM_OUT = 512
TM_BIN = 512
QB_SWA = 512


def _const_spec(shape):
    return pl.BlockSpec(shape, lambda *_: (0,) * len(shape), pipeline_mode=pl.Buffered(1))


def _params(n_axes):
    return pltpu.CompilerParams(
        dimension_semantics=("arbitrary",) * n_axes, vmem_limit_bytes=VMEM_LIMIT)


def _rms_rows(xf, g_row):
    ms = jnp.mean(xf * xf, axis=-1, keepdims=True)
    return xf * lax.rsqrt(ms + EPS) * g_row


def _dot(a, b):
    return jnp.dot(a, b, preferred_element_type=F32)


def _dot_nt(a, b):
    return lax.dot_general(a, b, (((1,), (1,)), ((), ())), preferred_element_type=F32)


def _silu(g):
    return g * jax.nn.sigmoid(g)


def _mla_in_kernel(x_ref, g_ref, wl_ref, wg_ref, gq_ref, gkv_ref,
                   cq_ref, ckv_ref, kr_ref, gate_ref):
    h = _rms_rows(x_ref[...], g_ref[...]).astype(BF16)
    lat = _dot(h, wl_ref[...])
    cq_ref[...] = _rms_rows(lat[:, :A_LORA], gq_ref[...]).astype(BF16)
    ckv_ref[...] = _rms_rows(lat[:, A_LORA:2 * A_LORA], gkv_ref[...]).astype(BF16)
    kr_ref[...] = lat[:, 2 * A_LORA:]
    n_chunk = 512
    for c in range(gate_ref.shape[1] // n_chunk):
        sl = slice(c * n_chunk, (c + 1) * n_chunk)
        gate_ref[:, sl] = _silu(_dot(h, wg_ref[:, sl])).astype(BF16)


def _mla_up_kernel(cq_ref, ckv_ref, kr_ref, posr_ref, invc_ref,
                   wqt_ref, wk_ref, wvt_ref, gqc_ref, gkn_ref, gkr_ref,
                   qt_ref, k_ref, vt_ref, *, q_scale):
    tm = cq_ref.shape[0]
    cq = cq_ref[...]
    ckv = ckv_ref[...]

    ang_t = invc_ref[...] * posr_ref[0]
    cos_t = jnp.cos(ang_t)
    sin_t = jnp.sin(ang_t)
    cos_c = cos_t[None]
    sin_c = sin_t[None]
    gqc = gqc_ref[...][None]
    hpc = 4
    half = A_ROPE // 2
    for c in range(A_HEADS // hpc):
        rows = slice(c * hpc * A_QK, (c + 1) * hpc * A_QK)
        qt = _dot_nt(wqt_ref[rows, :], cq).reshape(hpc, A_QK, tm)
        ss = jnp.sum(qt * qt, axis=1, keepdims=True)
        qn = qt * (lax.rsqrt(ss * (1.0 / A_QK) + EPS) * q_scale) * gqc
        x1 = qn[:, A_NOPE:A_NOPE + half]
        x2 = qn[:, A_NOPE + half:]
        full = jnp.concatenate(
            [qn[:, :A_NOPE], x1 * cos_c - x2 * sin_c, x2 * cos_c + x1 * sin_c,
             jnp.zeros((hpc, A_QK_PAD - A_QK, tm), F32)], axis=1)
        qt_ref[0, c * hpc:(c + 1) * hpc, 0] = full.astype(BF16)

    kr = kr_ref[...]
    ss_r = jnp.sum(kr * kr, axis=-1, keepdims=True)
    krt = (kr * gkr_ref[...]).T
    y1 = krt[:half]
    y2 = krt[half:A_ROPE]
    kr_rot = jnp.concatenate(
        [y1 * cos_t - y2 * sin_t, y2 * cos_t + y1 * sin_t, krt[A_ROPE:]], axis=0).T
    gkn = gkn_ref[...]
    kn_all = _dot(ckv, wk_ref[...])
    for hh in range(A_HEADS):
        kn = kn_all[:, hh * A_NOPE:(hh + 1) * A_NOPE]
        ss = jnp.sum(kn * kn, axis=-1, keepdims=True) + ss_r
        r = lax.rsqrt(ss * (1.0 / A_QK) + EPS)
        k_ref[0, hh, :, :A_NOPE] = (kn * gkn * r).astype(BF16)
        k_ref[0, hh, :, A_NOPE:] = (kr_rot * r).astype(BF16)

    for c in range(A_HEADS // hpc):
        rows = slice(c * hpc * A_V, (c + 1) * hpc * A_V)
        vt = _dot_nt(wvt_ref[rows, :], ckv)
        vt_ref[0, c * hpc:(c + 1) * hpc, 0, :A_V, :] = vt.reshape(hpc, A_V, tm).astype(BF16)
    ones_row = lax.broadcasted_iota(jnp.int32, (A_HEADS, A_V_PAD - A_V, tm), 1) == 0
    vt_ref[0, :, 0, A_V:, :] = jnp.where(ones_row, 1.0, 0.0).astype(BF16)


def _mla_attn_kernel(qt_ref, k_ref, vt_ref, gate_ref, o_ref,
                     s0_ref, s1_ref, acc_ref, m_ref, *, tq, tk, tc):
    seq = k_ref.shape[2]
    tv = vt_ref.shape[4]
    assert tq == 2 * tk and qt_ref.shape[4] == tc and tk % tv == 0
    n_c = tq // tc

    def scores(s_ref, ks, qs, c):
        cs = slice(c * tc, (c + 1) * tc)
        s_ref[:, cs] = _dot(k_ref[0, 0, pl.ds(ks, tk), :], qt_ref[0, 0, qs // tc + c])

    def softmax_pv(s_ref, ks, c, mask):
        cs = slice(c * tc, (c + 1) * tc)
        st = s_ref[:, cs]
        if mask is not None:
            st = jnp.where(mask, st, NEG)
        m = m_ref[:, cs]
        m_new = jnp.maximum(m, jnp.max(st, axis=0, keepdims=True))
        p = jnp.exp2(st - m_new).astype(BF16)
        alpha = jnp.exp2(m - m_new)
        m_ref[:, cs] = m_new
        vt = jnp.concatenate([vt_ref[0, 0, ks // tv + i] for i in range(tk // tv)], axis=1)
        acc_ref[:, cs] = alpha * acc_ref[:, cs] + _dot(vt, p)

    def q_body(qi, _):
        qs = pl.multiple_of(qi * tq, tq)

        def pair(j, _):
            ks = pl.multiple_of(j * tq, tq)
            for c in range(n_c):
                scores(s1_ref, ks + tk, qs, c)
                softmax_pv(s0_ref, ks, c, None)
            for c in range(n_c):
                scores(s0_ref, ks + tq, qs, c)
                softmax_pv(s1_ref, ks + tk, c, None)
            return 0

        def finish(c):
            cs = slice(c * tc, (c + 1) * tc)
            o = (acc_ref[:A_V, cs] * (1.0 / acc_ref[A_V:A_V + 1, cs])).T
            rows = pl.ds(qs + c * tc, tc)
            o_ref[rows, :] = (o * gate_ref[rows, :].astype(F32)).astype(BF16)

        acc_ref[...] = jnp.zeros_like(acc_ref)
        m_ref[...] = jnp.full_like(m_ref, NEG)
        one = qi & 1
        two = lax.shift_right_logical(qi, 1) & 1

        @pl.when(one == 1)
        def _():
            pair(0, 0)

        @pl.when(two == 1)
        def _():
            pair(one, 0)
            pair(one + 1, 0)

        def four_pairs(j4, _):
            base = one + 2 * two + 4 * j4
            for i in range(4):
                pair(base + i, 0)
            return 0

        lax.fori_loop(0, lax.shift_right_logical(qi, 2), four_pairs, 0)
        row = lax.broadcasted_iota(jnp.int32, (tk, tc), 0)
        col = lax.broadcasted_iota(jnp.int32, (tk, tc), 1)
        for c in range(n_c):
            if (c + 1) * tc > tk:
                scores(s1_ref, qs + tk, qs, c)
        for c in range(n_c):
            softmax_pv(s0_ref, qs, c, None if c * tc >= tk else row <= col + c * tc)
            if (c + 1) * tc <= tk:
                finish(c)
        qs_next = pl.multiple_of(jnp.minimum(qs + tq, seq - tq), tq)
        for c in range(n_c):
            scores(s0_ref, 0, qs_next, c)
        for c in range(n_c):
            if (c + 1) * tc > tk:
                softmax_pv(s1_ref, qs + tk, c, row + tk <= col + c * tc)
                finish(c)
        return 0

    for c in range(n_c):
        scores(s0_ref, 0, 0, c)
    lax.fori_loop(0, seq // tq, q_body, 0)


def _mla_out_kernel(og_ref, x_ref, wo_ref, gkv_ref, wkt_ref, wvt_ref, pos_ref, inv_ref, gk_ref,
                    x1_ref, k_ref, vt_ref):
    tm = x_ref.shape[0]
    x1 = x_ref[...] + _dot(og_ref[...], wo_ref[...])
    x1_ref[...] = x1
    hk = _rms_rows(x1, gkv_ref[...]).astype(BF16)
    kt = _dot_nt(wkt_ref[...], hk).reshape(B_KV_HEADS, B_HD, tm)
    ss = jnp.sum(kt * kt, axis=1, keepdims=True)
    kn = kt * lax.rsqrt(ss * (1.0 / B_HD) + EPS) * gk_ref[...][None]
    ang = inv_ref[...] * pos_ref[0]
    c = jnp.cos(ang)[None]
    s = jnp.sin(ang)[None]
    x1r = kn[:, :B_ROPE_HALF]
    x2r = kn[:, B_ROPE_HALF:2 * B_ROPE_HALF]
    kfull = jnp.concatenate(
        [x1r * c - x2r * s, x2r * c + x1r * s, kn[:, 2 * B_ROPE_HALF:],
         jnp.zeros((B_KV_HEADS, B_K_PAD - B_HD, tm), F32)], axis=1)
    k_ref[0] = kfull.reshape(B_KV_HEADS * B_K_PAD, tm).T.astype(BF16)
    vt = _dot_nt(wvt_ref[...], hk).reshape(B_KV_HEADS, B_HD, tm)
    vt_ref[0, :, :B_HD, :] = vt.astype(BF16)
    ones_row = lax.broadcasted_iota(jnp.int32, (B_KV_HEADS, B_V_PAD - B_HD, tm), 1) == 0
    vt_ref[0, :, B_HD:, :] = jnp.where(ones_row, 1.0, 0.0).astype(BF16)


def _swa_in_kernel(x1_ref, g_ref, wqt_ref, wg_ref, pos_ref, inv_ref, gq_ref,
                   qt_ref, gate_ref, *, q_scale):
    tm = x1_ref.shape[0]
    hb = _rms_rows(x1_ref[...], g_ref[...]).astype(BF16)
    ang = inv_ref[...] * pos_ref[0]
    c = jnp.cos(ang)[None]
    s = jnp.sin(ang)[None]
    gq = gq_ref[...][None]
    hpc = 8
    for ch in range(B_HEADS // hpc):
        rows = slice(ch * hpc * B_HD, (ch + 1) * hpc * B_HD)
        qt = _dot_nt(wqt_ref[rows, :], hb).reshape(hpc, B_HD, tm)
        ss = jnp.sum(qt * qt, axis=1, keepdims=True)
        qn = qt * (lax.rsqrt(ss * (1.0 / B_HD) + EPS) * q_scale) * gq
        x1r = qn[:, :B_ROPE_HALF]
        x2r = qn[:, B_ROPE_HALF:2 * B_ROPE_HALF]
        qfull = jnp.concatenate(
            [x1r * c - x2r * s, x2r * c + x1r * s, qn[:, 2 * B_ROPE_HALF:]], axis=1)
        qt_ref[0, rows, :] = qfull.reshape(hpc * B_HD, tm).astype(BF16)
    n_chunk = 512
    for cc in range(gate_ref.shape[1] // n_chunk):
        sl = slice(cc * n_chunk, (cc + 1) * n_chunk)
        gate_ref[:, sl] = _silu(_dot(hb, wg_ref[:, sl])).astype(BF16)


def _swa_attn_kernel(sink_ref, qt_ref, kc_ref, kp_ref, vc_ref, vp_ref, gate_ref, o_ref,
                     ot0_ref, ot1_ref):
    w = B_WINDOW
    n_blk = qt_ref.shape[2] // w
    j = pl.program_id(1)
    gw = B_GROUP * w
    row = lax.broadcasted_iota(jnp.int32, (w, gw), 0)
    col = lax.broadcasted_iota(jnp.int32, (w, gw), 1)
    cur_visible = row <= (col & (w - 1))
    head_in_group = lax.broadcasted_iota(jnp.int32, (1, gw), 1) // w
    has_prev = j > 0

    def band_scores(blk, g):
        qs = slice(blk * w, (blk + 1) * w)
        ksl = slice(g * B_K_PAD, g * B_K_PAD + B_HD)
        if blk == 0:
            k_prev = kp_ref[0, :, ksl]
        else:
            k_prev = kc_ref[0, (blk - 1) * w:blk * w, ksl]
        k_band = jnp.concatenate([k_prev, kc_ref[0, qs, ksl]], axis=0)
        qt_g = jnp.concatenate(
            [qt_ref[0, (B_GROUP * g + i) * B_HD:(B_GROUP * g + i + 1) * B_HD, qs]
             for i in range(B_GROUP)], axis=1)
        return _dot(k_band, qt_g)

    def softmax_pv(st, blk, g, ot_ref):
        qs = slice(blk * w, (blk + 1) * w)
        s_prev = st[:w]
        if blk == 0:
            s_prev = jnp.where(has_prev, s_prev, NEG)
        s = jnp.where(cur_visible, st[w:], s_prev)
        sink = jnp.full((1, gw), sink_ref[B_GROUP * g] * LOG2E, F32)
        for i in range(1, B_GROUP):
            sink = jnp.where(head_in_group == i, sink_ref[B_GROUP * g + i] * LOG2E, sink)
        m = jnp.maximum(jnp.max(s, axis=0, keepdims=True), sink)
        e = jnp.exp2(s - m)
        e_cur = jnp.where(cur_visible, e, 0.0)
        p = jnp.concatenate([e - e_cur, e_cur], axis=0).astype(BF16)
        if blk == 0:
            v_prev = vp_ref[0, g]
        else:
            v_prev = vc_ref[0, g, :, (blk - 1) * w:blk * w]
        v_band = jnp.concatenate([v_prev, vc_ref[0, g, :, qs]], axis=1)
        pv = _dot(v_band, p)
        den = pv[B_HD:B_HD + 1] + jnp.exp2(sink - m)
        ot = pv[:B_HD] * (1.0 / den)
        for i in range(B_GROUP):
            hh = B_GROUP * g + i
            ot_ref[hh * B_HD:(hh + 1) * B_HD, :] = ot[:, i * w:(i + 1) * w]

    order = [(blk, g) for blk in range(n_blk) for g in range(B_KV_HEADS)]
    st = band_scores(*order[0])
    for idx, (blk, g) in enumerate(order):
        st_next = band_scores(*order[idx + 1]) if idx + 1 < len(order) else None
        ot_ref = ot0_ref if blk % 2 == 0 else ot1_ref
        softmax_pv(st, blk, g, ot_ref)
        if g == B_KV_HEADS - 1:
            qs = slice(blk * w, (blk + 1) * w)
            o_ref[qs, :] = (ot_ref[...].T * gate_ref[qs, :].astype(F32)).astype(BF16)
        st = st_next


def _swa_out_kernel(og_ref, x1_ref, wo_ref, o_ref):
    o_ref[...] = x1_ref[...] + _dot(og_ref[...], wo_ref[...])


def kernel(x, positions, a_norm, a_w_in, a_q_latent_norm, a_w_q_up, a_kv_latent_norm, a_w_kv_up,
           a_q_head_norm, a_k_head_norm, a_w_o, kv_norm, kv_w, kv_k_head_norm, b_norm, b_w_in,
           b_q_head_norm, b_sinks, b_w_o):
    bsz, seq, d = x.shape
    t = bsz * seq
    assert a_norm.shape[0] == 1 and b_norm.shape[0] == 1
    assert seq % TQ_A == 0 and seq % QB_SWA == 0 and TQ_A % TK_A == 0

    xf = x.reshape(t, d)
    pos_row = positions.reshape(bsz, 1, seq).astype(F32)

    w_in = a_w_in[0]
    n_lat = 2 * A_LORA + A_ROPE
    wl = jnp.pad(w_in[:, :n_lat], ((0, 0), (0, 128 - A_ROPE))).astype(BF16)
    wg = w_in[:, n_lat:].astype(BF16)
    wqt = a_w_q_up[0].T.astype(BF16)
    wkv = a_w_kv_up[0].reshape(A_LORA, A_HEADS, A_NOPE + A_V)
    wk = wkv[:, :, :A_NOPE].reshape(A_LORA, A_HEADS * A_NOPE).astype(BF16)
    wvt = wkv[:, :, A_NOPE:].reshape(A_LORA, A_HEADS * A_V).T.astype(BF16)
    gq_col = jnp.broadcast_to(a_q_head_norm[0][:, None], (A_QK, TM_UP))
    gkn =a_k_head_norm[0][:A_NOPE].reshape(1, A_NOPE)
    gkr = jnp.pad(a_k_head_norm[0][A_NOPE:], (0, 128 - A_ROPE)).reshape(1, 128)
    inv_a = jnp.power(jnp.float32(ROPE_THETA), -jnp.arange(0, A_ROPE, 2, dtype=F32) / A_ROPE)
    inv_a_col = jnp.broadcast_to(inv_a[:, None], (A_ROPE // 2, TM_UP))
    wo_a = a_w_o[0].astype(BF16)

    kv_dim = B_KV_HEADS * B_HD
    wkt_b = kv_w[:, :kv_dim].T.astype(BF16)
    wvt_b = kv_w[:, kv_dim:].T.astype(BF16)
    rot_b = 2 * B_ROPE_HALF
    inv_b = jnp.power(jnp.float32(ROPE_THETA), -jnp.arange(0, rot_b, 2, dtype=F32) / rot_b)
    q_dim = B_HEADS * B_HD
    wqt_b = b_w_in[0][:, :q_dim].T.astype(BF16)
    wg_b = b_w_in[0][:, q_dim:].astype(BF16)
    wo_b = b_w_o[0].astype(BF16)

    n_in = t // TM_IN
    cq, ckv, kr, gate_a = pl.pallas_call(
        _mla_in_kernel,
        grid=(n_in,),
        in_specs=[
            pl.BlockSpec((TM_IN, d), lambda i: (i, 0)),
            _const_spec((1, d)),
            _const_spec(wl.shape),
            _const_spec(wg.shape),
            _const_spec((1, A_LORA)),
            _const_spec((1, A_LORA)),
        ],
        out_specs=[
            pl.BlockSpec((TM_IN, A_LORA), lambda i: (i, 0)),
            pl.BlockSpec((TM_IN, A_LORA), lambda i: (i, 0)),
            pl.BlockSpec((TM_IN, 128), lambda i: (i, 0)),
            pl.BlockSpec((TM_IN, d), lambda i: (i, 0)),
        ],
        out_shape=[
            jax.ShapeDtypeStruct((t, A_LORA), BF16),
            jax.ShapeDtypeStruct((t, A_LORA), BF16),
            jax.ShapeDtypeStruct((t, 128), F32),
            jax.ShapeDtypeStruct((t, A_HEADS * A_V), BF16),
        ],
        compiler_params=_params(1),
        name="mla_in",
    )(xf, a_norm[0].reshape(1, d), wl, wg, a_q_latent_norm[0].reshape(1, A_LORA),
      a_kv_latent_norm[0].reshape(1, A_LORA))

    ns_up = seq // TM_UP
    tok_up = lambda b, j: (b * ns_up + j, 0)
    qt_a, k_a, vt_a = pl.pallas_call(
        functools.partial(_mla_up_kernel, q_scale=A_QK ** -0.5 * LOG2E),
        grid=(bsz, ns_up),
        in_specs=[
            pl.BlockSpec((TM_UP, A_LORA), tok_up),
            pl.BlockSpec((TM_UP, A_LORA), tok_up),
            pl.BlockSpec((TM_UP, 128), tok_up),
            pl.BlockSpec((1, 1, TM_UP), lambda b, j: (b, 0, j)),
            _const_spec(inv_a_col.shape),
            _const_spec(wqt.shape),
            _const_spec(wk.shape),
            _const_spec(wvt.shape),
            _const_spec(gq_col.shape),
            _const_spec((1, A_NOPE)),
            _const_spec((1, 128)),
        ],
        out_specs=[
            pl.BlockSpec((1, A_HEADS, 1, A_QK_PAD, TM_UP), lambda b, j: (b, 0, j, 0, 0)),
            pl.BlockSpec((1, A_HEADS, TM_UP, A_QK_PAD), lambda b, j: (b, 0, j, 0)),
            pl.BlockSpec((1, A_HEADS, 1, A_V_PAD, TM_UP), lambda b, j: (b, 0, j, 0, 0)),
        ],
        out_shape=[
            jax.ShapeDtypeStruct((bsz, A_HEADS, ns_up, A_QK_PAD, TM_UP), BF16),
            jax.ShapeDtypeStruct((bsz, A_HEADS, seq, A_QK_PAD), BF16),
            jax.ShapeDtypeStruct((bsz, A_HEADS, ns_up, A_V_PAD, TM_UP), BF16),
        ],
        compiler_params=_params(2),
        name="mla_up",
    )(cq, ckv, kr, pos_row, inv_a_col, wqt, wk, wvt, gq_col, gkn, gkr)

    og_a = pl.pallas_call(
        functools.partial(_mla_attn_kernel, tq=TQ_A, tk=TK_A, tc=TC_A),
        grid=(bsz, A_HEADS),
        in_specs=[
            pl.BlockSpec((1, 1, ns_up, A_QK_PAD, TM_UP), lambda b, h: (b, h, 0, 0, 0)),
            pl.BlockSpec((1, 1, seq, A_QK_PAD), lambda b, h: (b, h, 0, 0)),
            pl.BlockSpec((1, 1, ns_up, A_V_PAD, TM_UP), lambda b, h: (b, h, 0, 0, 0)),
            pl.BlockSpec((seq, A_V), lambda b, h: (b, h)),
        ],
        out_specs=pl.BlockSpec((seq, A_V), lambda b, h: (b, h)),
        out_shape=jax.ShapeDtypeStruct((t, A_HEADS * A_V), BF16),
        scratch_shapes=[pltpu.VMEM((TK_A, TQ_A), F32), pltpu.VMEM((TK_A, TQ_A), F32),
                        pltpu.VMEM((A_V_PAD, TQ_A), F32), pltpu.VMEM((1, TQ_A), F32)],
        compiler_params=_params(2),
        name="mla_attn",
    )(qt_a, k_a, vt_a, gate_a)

    ns_out = seq // TM_OUT
    tok_out = lambda b, j: (b * ns_out + j, 0)
    x1, k_b, vt_b = pl.pallas_call(
        _mla_out_kernel,
        grid=(bsz, ns_out),
        in_specs=[
            pl.BlockSpec((TM_OUT, d), tok_out),
            pl.BlockSpec((TM_OUT, d), tok_out),
            _const_spec(wo_a.shape),
            _const_spec((1, d)),
            _const_spec(wkt_b.shape),
            _const_spec(wvt_b.shape),
            pl.BlockSpec((1, 1, TM_OUT), lambda b, j: (b, 0, j)),
            _const_spec((B_ROPE_HALF, TM_OUT)),
            _const_spec((B_HD, TM_OUT)),
        ],
        out_specs=[
            pl.BlockSpec((TM_OUT, d), tok_out),
            pl.BlockSpec((1, TM_OUT, B_KV_HEADS * B_K_PAD), lambda b, j: (b, j, 0)),
            pl.BlockSpec((1, B_KV_HEADS, B_V_PAD, TM_OUT), lambda b, j: (b, 0, 0, j)),
        ],
        out_shape=[
            jax.ShapeDtypeStruct((t, d), F32),
            jax.ShapeDtypeStruct((bsz, seq, B_KV_HEADS * B_K_PAD), BF16),
            jax.ShapeDtypeStruct((bsz, B_KV_HEADS, B_V_PAD, seq), BF16),
        ],
        compiler_params=_params(2),
        name="mla_out",
    )(og_a, xf, wo_a, kv_norm.reshape(1, d), wkt_b, wvt_b, pos_row,
      jnp.broadcast_to(inv_b[:, None], (B_ROPE_HALF, TM_OUT)),
      jnp.broadcast_to(kv_k_head_norm[:, None], (B_HD, TM_OUT)))

    ns_bin = seq // TM_BIN
    tok_bin = lambda b, j: (b * ns_bin + j, 0)
    qt_b, gate_b = pl.pallas_call(
        functools.partial(_swa_in_kernel, q_scale=B_HD ** -0.5 * LOG2E),
        grid=(bsz, ns_bin),
        in_specs=[
            pl.BlockSpec((TM_BIN, d), tok_bin),
            _const_spec((1, d)),
            _const_spec(wqt_b.shape),
            _const_spec(wg_b.shape),
            pl.BlockSpec((1, 1, TM_BIN), lambda b, j: (b, 0, j)),
            _const_spec((B_ROPE_HALF, TM_BIN)),
            _const_spec((B_HD, TM_BIN)),
        ],
        out_specs=[
            pl.BlockSpec((1, q_dim, TM_BIN), lambda b, j: (b, 0, j)),
            pl.BlockSpec((TM_BIN, q_dim), tok_bin),
        ],
        out_shape=[
            jax.ShapeDtypeStruct((bsz, q_dim, seq), BF16),
            jax.ShapeDtypeStruct((t, q_dim), BF16),
        ],
        compiler_params=_params(2),
        name="swa_in",
    )(x1, b_norm[0].reshape(1, d), wqt_b, wg_b, pos_row,
      jnp.broadcast_to(inv_b[:, None], (B_ROPE_HALF, TM_BIN)),
      jnp.broadcast_to(b_q_head_norm[0][:, None], (B_HD, TM_BIN)))

    ns_sw = seq // QB_SWA
    blk_per_step = QB_SWA // B_WINDOW
    prev_blk = lambda b, j: jnp.maximum(j * blk_per_step - 1, 0)
    og_b = pl.pallas_call(
        _swa_attn_kernel,
        grid=(bsz, ns_sw),
        in_specs=[
            pl.BlockSpec(memory_space=pltpu.SMEM),
            pl.BlockSpec((1, q_dim, QB_SWA), lambda b, j: (b, 0, j)),
            pl.BlockSpec((1, QB_SWA, B_KV_HEADS * B_K_PAD), lambda b, j: (b, j, 0)),
            pl.BlockSpec((1, B_WINDOW, B_KV_HEADS * B_K_PAD), lambda b, j: (b, prev_blk(b, j), 0)),
            pl.BlockSpec((1, B_KV_HEADS, B_V_PAD, QB_SWA), lambda b, j: (b, 0, 0, j)),
            pl.BlockSpec((1, B_KV_HEADS, B_V_PAD, B_WINDOW), lambda b, j: (b, 0, 0, prev_blk(b, j))),
            pl.BlockSpec((QB_SWA, q_dim), lambda b, j: (b * ns_sw + j, 0)),
        ],
        out_specs=pl.BlockSpec((QB_SWA, q_dim), lambda b, j: (b * ns_sw + j, 0)),
        out_shape=jax.ShapeDtypeStruct((t, q_dim), BF16),
        scratch_shapes=[pltpu.VMEM((q_dim, B_WINDOW), F32), pltpu.VMEM((q_dim, B_WINDOW), F32)],
        compiler_params=_params(2),
        name="swa_attn",
    )(b_sinks[0], qt_b, k_b, k_b, vt_b, vt_b, gate_b)

    n_out = t // TM_OUT
    out = pl.pallas_call(
        _swa_out_kernel,
        grid=(n_out,),
        in_specs=[
            pl.BlockSpec((TM_OUT, q_dim), lambda i: (i, 0)),
            pl.BlockSpec((TM_OUT, d), lambda i: (i, 0)),
            _const_spec(wo_b.shape),
        ],
        out_specs=pl.BlockSpec((TM_OUT, d), lambda i: (i, 0)),
        out_shape=jax.ShapeDtypeStruct((t, d), F32),
        compiler_params=_params(1),
        name="swa_out",
    )(og_b, x1, wo_b)
    return out.reshape(bsz, seq, d)
```

```python
import functools
import math

import jax
import jax.numpy as jnp
from jax import lax
from jax.experimental import pallas as pl
from jax.experimental.pallas import tpu as pltpu

F32 = jnp.float32
BF16 = jnp.bfloat16

EPS = 1e-6
ROPE_THETA = 500000.0
NEG = -1e30
LOG2E = math.log2(math.e)

A_HEADS = 16
A_LORA = 512
A_NOPE = 128
A_ROPE = 64
A_QK = A_NOPE + A_ROPE
A_QK_PAD = 256
A_V = 128
A_V_PAD = A_V + 16

B_HEADS = 32
B_KV_HEADS = 8
B_GROUP = B_HEADS // B_KV_HEADS
B_HD = 64
B_ROPE_HALF = 8
B_WINDOW = 128
B_K_PAD = 128
B_V_PAD = B_HD + 16

VMEM_LIMIT = 56 * 1024 * 1024

TM_UP = 256
TQ_A = 1024
TK_A = 512
TC_A = 256
TM_OUT = 512
TM_BIN = 512
QB_SWA = 512


def _const_spec(shape):
    return pl.BlockSpec(shape, lambda *_: (0,) * len(shape), pipeline_mode=pl.Buffered(1))


def _params(n_axes):
    return pltpu.CompilerParams(
        dimension_semantics=("arbitrary",) * n_axes, vmem_limit_bytes=VMEM_LIMIT)


def _rms_rows(xf, g_row):
    ms = jnp.mean(xf * xf, axis=-1, keepdims=True)
    return xf * lax.rsqrt(ms + EPS) * g_row


def _dot(a, b):
    return jnp.dot(a, b, preferred_element_type=F32)


def _dot_nt(a, b):
    return lax.dot_general(a, b, (((1,), (1,)), ((), ())), preferred_element_type=F32)


def _silu(g):
    return g * jax.nn.sigmoid(g)


def _mla_proj_kernel(x_ref, g_ref, wl_ref, wg_ref, glq_ref, glkv_ref, posr_ref, invc_ref,
                     wqt_ref, wk_ref, wvt_ref, gqc_ref, gkn_ref, gkr_ref,
                     qt_ref, k_ref, vt_ref, gate_ref, *, q_scale):
    tm = x_ref.shape[0]
    h = _rms_rows(x_ref[...], g_ref[...]).astype(BF16)
    lat = _dot(h, wl_ref[...])
    cq = _rms_rows(lat[:, :A_LORA], glq_ref[...]).astype(BF16)
    ckv = _rms_rows(lat[:, A_LORA:2 * A_LORA], glkv_ref[...]).astype(BF16)
    kr = lat[:, 2 * A_LORA:]

    ang_t = invc_ref[...] * posr_ref[0]
    cos_t = jnp.cos(ang_t)
    sin_t = jnp.sin(ang_t)
    cos_c = cos_t[None]
    sin_c = sin_t[None]
    gqc = gqc_ref[...][None]
    hpc = 4
    n_chunk = A_HEADS // hpc
    g_cols = gate_ref.shape[1] // n_chunk
    half = A_ROPE // 2
    for c in range(n_chunk):
        rows = slice(c * hpc * A_QK, (c + 1) * hpc * A_QK)
        qt = _dot_nt(wqt_ref[rows, :], cq).reshape(hpc, A_QK, tm)
        ss = jnp.sum(qt * qt, axis=1, keepdims=True)
        qn = qt * (lax.rsqrt(ss * (1.0 / A_QK) + EPS) * q_scale) * gqc
        x1 = qn[:, A_NOPE:A_NOPE + half]
        x2 = qn[:, A_NOPE + half:]
        full = jnp.concatenate(
            [qn[:, :A_NOPE], x1 * cos_c - x2 * sin_c, x2 * cos_c + x1 * sin_c,
             jnp.zeros((hpc, A_QK_PAD - A_QK, tm), F32)], axis=1)
        qt_ref[0, c * hpc:(c + 1) * hpc, 0] = full.astype(BF16)
        sl = slice(c * g_cols, (c + 1) * g_cols)
        gate_ref[:, sl] = _silu(_dot(h, wg_ref[:, sl])).astype(BF16)

    ss_r =jnp.sum(kr * kr, axis=-1, keepdims=True)
    krt = (kr * gkr_ref[...]).T
    y1 = krt[:half]
    y2 = krt[half:A_ROPE]
    kr_rot = jnp.concatenate(
        [y1 * cos_t - y2 * sin_t, y2 * cos_t + y1 * sin_t, krt[A_ROPE:]], axis=0).T
    gkn = gkn_ref[...]
    kn_all = _dot(ckv, wk_ref[...])
    for hh in range(A_HEADS):
        kn = kn_all[:, hh * A_NOPE:(hh + 1) * A_NOPE]
        ss = jnp.sum(kn * kn, axis=-1, keepdims=True) + ss_r
        r = lax.rsqrt(ss * (1.0 / A_QK) + EPS)
        k_ref[0, hh, :, :A_NOPE] = (kn * gkn * r).astype(BF16)
        k_ref[0, hh, :, A_NOPE:] = (kr_rot * r).astype(BF16)

    for c in range(n_chunk):
        rows = slice(c * hpc * A_V, (c + 1) * hpc * A_V)
        vt = _dot_nt(wvt_ref[rows, :], ckv)
        vt_ref[0, c * hpc:(c + 1) * hpc, 0, :A_V, :] = vt.reshape(hpc, A_V, tm).astype(BF16)
    ones_row = lax.broadcasted_iota(jnp.int32, (A_HEADS, A_V_PAD - A_V, tm), 1) == 0
    vt_ref[0, :, 0, A_V:, :] = jnp.where(ones_row, 1.0, 0.0).astype(BF16)


def _mla_attn_kernel(qt_ref, k_ref, vt_ref, gate_ref, o_ref,
                     s0_ref, s1_ref, acc_ref, m_ref, *, tq, tk, tc):
    seq = k_ref.shape[2]
    tv = vt_ref.shape[4]
    assert tq == 2 * tk and qt_ref.shape[4] == tc and tk % tv == 0
    n_c = tq // tc

    def scores(s_ref, ks, qs, c):
        cs = slice(c * tc, (c + 1) * tc)
        s_ref[:, cs] = _dot(k_ref[0, 0, pl.ds(ks, tk), :], qt_ref[0, 0, qs // tc + c])

    def softmax_pv(s_ref, ks, c, mask):
        cs = slice(c * tc, (c + 1) * tc)
        st = s_ref[:, cs]
        if mask is not None:
            st = jnp.where(mask, st, NEG)
        m = m_ref[:, cs]
        m_new = jnp.maximum(m, jnp.max(st, axis=0, keepdims=True))
        p = jnp.exp2(st - m_new).astype(BF16)
        alpha = jnp.exp2(m - m_new)
        m_ref[:, cs] = m_new
        vt = jnp.concatenate([vt_ref[0, 0, ks // tv + i] for i in range(tk // tv)], axis=1)
        acc_ref[:, cs] = alpha * acc_ref[:, cs] + _dot(vt, p)

    def q_body(qi, _):
        qs = pl.multiple_of(qi * tq, tq)

        def pair(j, _):
            ks = pl.multiple_of(j * tq, tq)
            for c in range(n_c):
                scores(s1_ref, ks + tk, qs, c)
                softmax_pv(s0_ref, ks, c, None)
            for c in range(n_c):
                scores(s0_ref, ks + tq, qs, c)
                softmax_pv(s1_ref, ks + tk, c, None)
            return 0

        def finish(c):
            cs = slice(c * tc, (c + 1) * tc)
            o = (acc_ref[:A_V, cs] * (1.0 / acc_ref[A_V:A_V + 1, cs])).T
            rows = pl.ds(qs + c * tc, tc)
            o_ref[rows, :] = (o * gate_ref[rows, :].astype(F32)).astype(BF16)

        acc_ref[...] = jnp.zeros_like(acc_ref)
        m_ref[...] = jnp.full_like(m_ref, NEG)
        one = qi & 1
        two = lax.shift_right_logical(qi, 1) & 1

        @pl.when(one == 1)
        def _():
            pair(0, 0)

        @pl.when(two == 1)
        def _():
            pair(one, 0)
            pair(one + 1, 0)

        def four_pairs(j4, _):
            base = one + 2 * two + 4 * j4
            for i in range(4):
                pair(base + i, 0)
            return 0

        lax.fori_loop(0, lax.shift_right_logical(qi, 2), four_pairs, 0)
        row = lax.broadcasted_iota(jnp.int32, (tk, tc), 0)
        col = lax.broadcasted_iota(jnp.int32, (tk, tc), 1)
        for c in range(n_c):
            if (c + 1) * tc > tk:
                scores(s1_ref, qs + tk, qs, c)
        for c in range(n_c):
            softmax_pv(s0_ref, qs, c, None if c * tc >= tk else row <= col + c * tc)
            if (c + 1) * tc <= tk:
                finish(c)
        qs_next = pl.multiple_of(jnp.minimum(qs + tq, seq - tq), tq)
        for c in range(n_c):
            scores(s0_ref, 0, qs_next, c)
        for c in range(n_c):
            if (c + 1) * tc > tk:
                softmax_pv(s1_ref, qs + tk, c, row + tk <= col + c * tc)
                finish(c)
        return 0

    for c in range(n_c):
        scores(s0_ref, 0, 0, c)
    lax.fori_loop(0, seq // tq, q_body, 0)


def _mla_out_kernel(og_ref, x_ref, wo_ref, gkv_ref, wkt_ref, wvt_ref, pos_ref, inv_ref, gk_ref,
                    x1_ref, k_ref, vt_ref):
    tm = x_ref.shape[0]
    x1 = x_ref[...] + _dot(og_ref[...], wo_ref[...])
    x1_ref[...] = x1
    hk = _rms_rows(x1, gkv_ref[...]).astype(BF16)
    kt = _dot_nt(wkt_ref[...], hk).reshape(B_KV_HEADS, B_HD, tm)
    ss = jnp.sum(kt * kt, axis=1, keepdims=True)
    kn = kt * lax.rsqrt(ss * (1.0 / B_HD) + EPS) * gk_ref[...][None]
    ang = inv_ref[...] * pos_ref[0]
    c = jnp.cos(ang)[None]
    s = jnp.sin(ang)[None]
    x1r = kn[:, :B_ROPE_HALF]
    x2r = kn[:, B_ROPE_HALF:2 * B_ROPE_HALF]
    kfull = jnp.concatenate(
        [x1r * c - x2r * s, x2r * c + x1r * s, kn[:, 2 * B_ROPE_HALF:],
         jnp.zeros((B_KV_HEADS, B_K_PAD - B_HD, tm), F32)], axis=1)
    k_ref[0] = kfull.reshape(B_KV_HEADS * B_K_PAD, tm).T.astype(BF16)
    vt = _dot_nt(wvt_ref[...], hk).reshape(B_KV_HEADS, B_HD, tm)
    vt_ref[0, :, :B_HD, :] = vt.astype(BF16)
    ones_row = lax.broadcasted_iota(jnp.int32, (B_KV_HEADS, B_V_PAD - B_HD, tm), 1) == 0
    vt_ref[0, :, B_HD:, :] = jnp.where(ones_row, 1.0, 0.0).astype(BF16)


def _swa_in_kernel(x1_ref, g_ref, wqt_ref, wg_ref, pos_ref, inv_ref, gq_ref,
                   qt_ref, gate_ref, *, q_scale):
    tm = x1_ref.shape[0]
    hb = _rms_rows(x1_ref[...], g_ref[...]).astype(BF16)
    ang = inv_ref[...] * pos_ref[0]
    c = jnp.cos(ang)[None]
    s = jnp.sin(ang)[None]
    gq = gq_ref[...][None]
    hpc = 8
    for ch in range(B_HEADS // hpc):
        rows = slice(ch * hpc * B_HD, (ch + 1) * hpc * B_HD)
        qt = _dot_nt(wqt_ref[rows, :], hb).reshape(hpc, B_HD, tm)
        ss = jnp.sum(qt * qt, axis=1, keepdims=True)
        qn = qt * (lax.rsqrt(ss * (1.0 / B_HD) + EPS) * q_scale) * gq
        x1r = qn[:, :B_ROPE_HALF]
        x2r = qn[:, B_ROPE_HALF:2 * B_ROPE_HALF]
        qfull = jnp.concatenate(
            [x1r * c - x2r * s, x2r * c + x1r * s, qn[:, 2 * B_ROPE_HALF:]], axis=1)
        qt_ref[0, rows, :] = qfull.reshape(hpc * B_HD, tm).astype(BF16)
    n_chunk = 512
    for cc in range(gate_ref.shape[1] // n_chunk):
        sl = slice(cc * n_chunk, (cc + 1) * n_chunk)
        gate_ref[:, sl] = _silu(_dot(hb, wg_ref[:, sl])).astype(BF16)


def _swa_attn_kernel(sink_ref, qt_ref, kc_ref, kp_ref, vc_ref, vp_ref, gate_ref, o_ref,
                     ot0_ref, ot1_ref):
    w = B_WINDOW
    n_blk = qt_ref.shape[2] // w
    j = pl.program_id(1)
    gw = B_GROUP * w
    row = lax.broadcasted_iota(jnp.int32, (w, gw), 0)
    col = lax.broadcasted_iota(jnp.int32, (w, gw), 1)
    cur_visible = row <= (col & (w - 1))
    cur_visible_f = cur_visible.astype(F32)
    head_in_group = lax.broadcasted_iota(jnp.int32, (1, gw), 1) // w
    has_prev = j > 0

    def band_scores(blk, g):
        qs = slice(blk * w, (blk + 1) * w)
        ksl = slice(g * B_K_PAD, g * B_K_PAD + B_HD)
        if blk == 0:
            k_prev = kp_ref[0, :, ksl]
        else:
            k_prev = kc_ref[0, (blk - 1) * w:blk * w, ksl]
        k_band = jnp.concatenate([k_prev, kc_ref[0, qs, ksl]], axis=0)
        qt_g = jnp.concatenate(
            [qt_ref[0, (B_GROUP * g + i) * B_HD:(B_GROUP * g + i + 1) * B_HD, qs]
             for i in range(B_GROUP)], axis=1)
        return _dot(k_band, qt_g)

    def softmax_pv(st, blk, g, ot_ref):
        qs = slice(blk * w, (blk + 1) * w)
        s_prev = st[:w]
        if blk == 0:
            s_prev = jnp.where(has_prev, s_prev, NEG)
        s = jnp.where(cur_visible, st[w:], s_prev)
        sink = jnp.full((1, gw), sink_ref[B_GROUP * g] * LOG2E, F32)
        for i in range(1, B_GROUP):
            sink = jnp.where(head_in_group == i, sink_ref[B_GROUP * g + i] * LOG2E, sink)
        m = jnp.maximum(jnp.max(s, axis=0, keepdims=True), sink)
        e = jnp.exp2(s - m)
        e_cur = e * cur_visible_f
        p = jnp.concatenate([e - e_cur, e_cur], axis=0).astype(BF16)
        if blk == 0:
            v_prev = vp_ref[0, g]
        else:
            v_prev = vc_ref[0, g, :, (blk - 1) * w:blk * w]
        v_band = jnp.concatenate([v_prev, vc_ref[0, g, :, qs]], axis=1)
        pv = _dot(v_band, p)
        den = pv[B_HD:B_HD + 1] + jnp.exp2(sink - m)
        ot = pv[:B_HD] * (1.0 / den)
        for i in range(B_GROUP):
            hh = B_GROUP * g + i
            ot_ref[hh * B_HD:(hh + 1) * B_HD, :] = ot[:, i * w:(i + 1) * w]

    order = [(blk, g) for blk in range(n_blk) for g in range(B_KV_HEADS)]
    st = band_scores(*order[0])
    for idx, (blk, g) in enumerate(order):
        st_next = band_scores(*order[idx + 1]) if idx + 1 < len(order) else None
        ot_ref = ot0_ref if blk % 2 == 0 else ot1_ref
        softmax_pv(st, blk, g, ot_ref)
        if g == B_KV_HEADS - 1:
            qs = slice(blk * w, (blk + 1) * w)
            o_ref[qs, :] = (ot_ref[...].T * gate_ref[qs, :].astype(F32)).astype(BF16)
        st = st_next


def _swa_out_kernel(og_ref, x1_ref, wo_ref, o_ref):
    o_ref[...] = x1_ref[...] + _dot(og_ref[...], wo_ref[...])


def kernel(x, positions, a_norm, a_w_in, a_q_latent_norm, a_w_q_up, a_kv_latent_norm, a_w_kv_up,
           a_q_head_norm, a_k_head_norm, a_w_o, kv_norm, kv_w, kv_k_head_norm, b_norm, b_w_in,
           b_q_head_norm, b_sinks, b_w_o):
    bsz, seq, d = x.shape
    t = bsz * seq
    assert a_norm.shape[0] == 1 and b_norm.shape[0] == 1
    assert seq % TQ_A == 0 and seq % QB_SWA == 0 and TQ_A % TK_A == 0

    xf = x.reshape(t, d)
    pos_row = positions.reshape(bsz, 1, seq).astype(F32)

    w_in = a_w_in[0]
    n_lat = 2 * A_LORA + A_ROPE
    wl = jnp.pad(w_in[:, :n_lat], ((0, 0), (0, 128 - A_ROPE))).astype(BF16)
    wg = w_in[:, n_lat:].astype(BF16)
    wqt = a_w_q_up[0].astype(BF16).T
    wkv = a_w_kv_up[0].astype(BF16).reshape(A_LORA, A_HEADS, A_NOPE + A_V)
    wk = wkv[:, :, :A_NOPE].reshape(A_LORA, A_HEADS * A_NOPE)
    wvt = wkv[:, :, A_NOPE:].reshape(A_LORA, A_HEADS * A_V).T
    gq_col = jnp.broadcast_to(a_q_head_norm[0][:, None], (A_QK, TM_UP))
    gkn = a_k_head_norm[0][:A_NOPE].reshape(1, A_NOPE)
    gkr = jnp.pad(a_k_head_norm[0][A_NOPE:], (0, 128 - A_ROPE)).reshape(1, 128)
    inv_a = jnp.power(jnp.float32(ROPE_THETA), -jnp.arange(0, A_ROPE, 2, dtype=F32) / A_ROPE)
    inv_a_col = jnp.broadcast_to(inv_a[:, None], (A_ROPE // 2, TM_UP))
    wo_a = a_w_o[0].astype(BF16)

    kv_dim = B_KV_HEADS * B_HD
    kv_w16 = kv_w.astype(BF16)
    wkt_b = kv_w16[:, :kv_dim].T
    wvt_b = kv_w16[:, kv_dim:].T
    rot_b = 2 * B_ROPE_HALF
    inv_b = jnp.power(jnp.float32(ROPE_THETA), -jnp.arange(0, rot_b, 2, dtype=F32) / rot_b)
    q_dim = B_HEADS * B_HD
    b_w_in16 = b_w_in[0].astype(BF16)
    wqt_b = b_w_in16[:, :q_dim].T
    wg_b = b_w_in16[:, q_dim:]
    wo_b = b_w_o[0].astype(BF16)

    ns_up = seq // TM_UP
    tok_up = lambda b, j: (b * ns_up + j, 0)
    qt_a, k_a, vt_a, gate_a = pl.pallas_call(
        functools.partial(_mla_proj_kernel, q_scale=A_QK ** -0.5 * LOG2E),
        grid=(bsz, ns_up),
        in_specs=[
            pl.BlockSpec((TM_UP, d), tok_up),
            _const_spec((1, d)),
            _const_spec(wl.shape),
            _const_spec(wg.shape),
            _const_spec((1, A_LORA)),
            _const_spec((1, A_LORA)),
            pl.BlockSpec((1, 1, TM_UP), lambda b, j: (b, 0, j)),
            _const_spec(inv_a_col.shape),
            _const_spec(wqt.shape),
            _const_spec(wk.shape),
            _const_spec(wvt.shape),
            _const_spec(gq_col.shape),
            _const_spec((1, A_NOPE)),
            _const_spec((1, 128)),
        ],
        out_specs=[
            pl.BlockSpec((1, A_HEADS, 1, A_QK_PAD, TM_UP), lambda b, j: (b, 0, j, 0, 0)),
            pl.BlockSpec((1, A_HEADS, TM_UP, A_QK_PAD), lambda b, j: (b, 0, j, 0)),
            pl.BlockSpec((1, A_HEADS, 1, A_V_PAD, TM_UP), lambda b, j: (b, 0, j, 0, 0)),
            pl.BlockSpec((TM_UP, d), tok_up),
        ],
        out_shape=[
            jax.ShapeDtypeStruct((bsz, A_HEADS, ns_up, A_QK_PAD, TM_UP), BF16),
            jax.ShapeDtypeStruct((bsz, A_HEADS, seq, A_QK_PAD), BF16),
            jax.ShapeDtypeStruct((bsz, A_HEADS, ns_up, A_V_PAD, TM_UP), BF16),
            jax.ShapeDtypeStruct((t, A_HEADS * A_V), BF16),
        ],
        compiler_params=_params(2),
        name="mla_proj",
    )(xf, a_norm[0].reshape(1, d), wl, wg, a_q_latent_norm[0].reshape(1, A_LORA),
      a_kv_latent_norm[0].reshape(1, A_LORA), pos_row, inv_a_col, wqt, wk, wvt, gq_col, gkn, gkr)

    og_a = pl.pallas_call(
        functools.partial(_mla_attn_kernel, tq=TQ_A, tk=TK_A, tc=TC_A),
        grid=(bsz, A_HEADS),
        in_specs=[
            pl.BlockSpec((1, 1, ns_up, A_QK_PAD, TM_UP), lambda b, h: (b, h, 0, 0, 0)),
            pl.BlockSpec((1, 1, seq, A_QK_PAD), lambda b, h: (b, h, 0, 0)),
            pl.BlockSpec((1, 1, ns_up, A_V_PAD, TM_UP), lambda b, h: (b, h, 0, 0, 0)),
            pl.BlockSpec((seq, A_V), lambda b, h: (b, h)),
        ],
        out_specs=pl.BlockSpec((seq, A_V), lambda b, h: (b, h)),
        out_shape=jax.ShapeDtypeStruct((t, A_HEADS * A_V), BF16),
        scratch_shapes=[pltpu.VMEM((TK_A, TQ_A), F32), pltpu.VMEM((TK_A, TQ_A), F32),
                        pltpu.VMEM((A_V_PAD, TQ_A), F32), pltpu.VMEM((1, TQ_A), F32)],
        compiler_params=_params(2),
        name="mla_attn",
    )(qt_a, k_a, vt_a, gate_a)

    ns_out = seq // TM_OUT
    tok_out = lambda b, j: (b * ns_out + j, 0)
    x1, k_b, vt_b = pl.pallas_call(
        _mla_out_kernel,
        grid=(bsz, ns_out),
        in_specs=[
            pl.BlockSpec((TM_OUT, d), tok_out),
            pl.BlockSpec((TM_OUT, d), tok_out),
            _const_spec(wo_a.shape),
            _const_spec((1, d)),
            _const_spec(wkt_b.shape),
            _const_spec(wvt_b.shape),
            pl.BlockSpec((1, 1, TM_OUT), lambda b, j: (b, 0, j)),
            _const_spec((B_ROPE_HALF, TM_OUT)),
            _const_spec((B_HD, TM_OUT)),
        ],
        out_specs=[
            pl.BlockSpec((TM_OUT, d), tok_out),
            pl.BlockSpec((1, TM_OUT, B_KV_HEADS * B_K_PAD), lambda b, j: (b, j, 0)),
            pl.BlockSpec((1, B_KV_HEADS, B_V_PAD, TM_OUT), lambda b, j: (b, 0, 0, j)),
        ],
        out_shape=[
            jax.ShapeDtypeStruct((t, d), F32),
            jax.ShapeDtypeStruct((bsz, seq, B_KV_HEADS * B_K_PAD), BF16),
            jax.ShapeDtypeStruct((bsz, B_KV_HEADS, B_V_PAD, seq), BF16),
        ],
        compiler_params=_params(2),
        name="mla_out",
    )(og_a, xf, wo_a, kv_norm.reshape(1, d), wkt_b, wvt_b, pos_row,
      jnp.broadcast_to(inv_b[:, None], (B_ROPE_HALF, TM_OUT)),
      jnp.broadcast_to(kv_k_head_norm[:, None], (B_HD, TM_OUT)))

    ns_bin = seq // TM_BIN
    tok_bin = lambda b, j: (b * ns_bin + j, 0)
    qt_b, gate_b = pl.pallas_call(
        functools.partial(_swa_in_kernel, q_scale=B_HD ** -0.5 * LOG2E),
        grid=(bsz, ns_bin),
        in_specs=[
            pl.BlockSpec((TM_BIN, d), tok_bin),
            _const_spec((1, d)),
            _const_spec(wqt_b.shape),
            _const_spec(wg_b.shape),
            pl.BlockSpec((1, 1, TM_BIN), lambda b, j: (b, 0, j)),
            _const_spec((B_ROPE_HALF, TM_BIN)),
            _const_spec((B_HD, TM_BIN)),
        ],
        out_specs=[
            pl.BlockSpec((1, q_dim, TM_BIN), lambda b, j: (b, 0, j)),
            pl.BlockSpec((TM_BIN, q_dim), tok_bin),
        ],
        out_shape=[
            jax.ShapeDtypeStruct((bsz, q_dim, seq), BF16),
            jax.ShapeDtypeStruct((t, q_dim), BF16),
        ],
        compiler_params=_params(2),
        name="swa_in",
    )(x1, b_norm[0].reshape(1, d), wqt_b, wg_b, pos_row,
      jnp.broadcast_to(inv_b[:, None], (B_ROPE_HALF, TM_BIN)),
      jnp.broadcast_to(b_q_head_norm[0][:, None], (B_HD, TM_BIN)))

    ns_sw = seq // QB_SWA
    blk_per_step = QB_SWA // B_WINDOW
    prev_blk = lambda b, j: jnp.maximum(j * blk_per_step - 1, 0)
    og_b = pl.pallas_call(
        _swa_attn_kernel,
        grid=(bsz, ns_sw),
        in_specs=[
            pl.BlockSpec(memory_space=pltpu.SMEM),
            pl.BlockSpec((1, q_dim, QB_SWA), lambda b, j: (b, 0, j)),
            pl.BlockSpec((1, QB_SWA, B_KV_HEADS * B_K_PAD), lambda b, j: (b, j, 0)),
            pl.BlockSpec((1, B_WINDOW, B_KV_HEADS * B_K_PAD), lambda b, j: (b, prev_blk(b, j), 0)),
            pl.BlockSpec((1, B_KV_HEADS, B_V_PAD, QB_SWA), lambda b, j: (b, 0, 0, j)),
            pl.BlockSpec((1, B_KV_HEADS, B_V_PAD, B_WINDOW), lambda b, j: (b, 0, 0, prev_blk(b, j))),
            pl.BlockSpec((QB_SWA, q_dim), lambda b, j: (b * ns_sw + j, 0)),
        ],
        out_specs=pl.BlockSpec((QB_SWA, q_dim), lambda b, j: (b * ns_sw + j, 0)),
        out_shape=jax.ShapeDtypeStruct((t, q_dim), BF16),
        scratch_shapes=[pltpu.VMEM((q_dim, B_WINDOW), F32), pltpu.VMEM((q_dim, B_WINDOW), F32)],
        compiler_params=_params(2),
        name="swa_attn",
    )(b_sinks[0], qt_b, k_b, k_b, vt_b, vt_b, gate_b)

    n_out = t // TM_OUT
    out = pl.pallas_call(
        _swa_out_kernel,
        grid=(n_out,),
        in_specs=[
            pl.BlockSpec((TM_OUT, q_dim), lambda i: (i, 0)),
            pl.BlockSpec((TM_OUT, d), lambda i: (i, 0)),
            _const_spec(wo_b.shape),
        ],
        out_specs=pl.BlockSpec((TM_OUT, d), lambda i: (i, 0)),
        out_shape=jax.ShapeDtypeStruct((t, d), F32),
        compiler_params=_params(1),
        name="swa_out",
    )(og_b, x1, wo_b)
    return out.reshape(bsz, seq, d)
```

```python
import functools
import math

import jax
import jax.numpy as jnp
from jax import lax
from jax.experimental import pallas as pl
from jax.experimental.pallas import tpu as pltpu

F32 = jnp.float32
BF16 = jnp.bfloat16

EPS = 1e-6
ROPE_THETA = 500000.0
NEG = -1e30
LOG2E = math.log2(math.e)

A_HEADS = 16
A_LORA = 512
A_NOPE = 128
A_ROPE = 64
A_QK = A_NOPE + A_ROPE
A_QK_PAD = 256
A_V = 128
A_V_PAD = A_V + 16

B_HEADS = 32
B_KV_HEADS = 8
B_GROUP = B_HEADS // B_KV_HEADS
B_HD = 64
B_ROPE_HALF = 8
B_WINDOW = 128
B_K_PAD = 128
B_V_PAD = B_HD + 16

VMEM_LIMIT = 56 * 1024 * 1024

TM_UP = 256
TQ_A = 1024
TK_A = 512
TC_A = 256
TM_OUT = 512
QB_SWA = 512


def _const_spec(shape):
    return pl.BlockSpec(shape, lambda *_: (0,) * len(shape), pipeline_mode=pl.Buffered(1))


def _params(n_axes):
    return pltpu.CompilerParams(
        dimension_semantics=("arbitrary",) * n_axes, vmem_limit_bytes=VMEM_LIMIT)


def _rms_rows(xf, g_row):
    ms = jnp.mean(xf * xf, axis=-1, keepdims=True)
    return xf * lax.rsqrt(ms + EPS) * g_row


def _dot(a, b):
    return jnp.dot(a, b, preferred_element_type=F32)


def _dot_nt(a, b):
    return lax.dot_general(a, b, (((1,), (1,)), ((), ())), preferred_element_type=F32)


def _silu(g):
    return g * jax.nn.sigmoid(g)


def _mla_proj_kernel(x_ref, g_ref, wl_ref, wg_ref, glq_ref, glkv_ref, posr_ref, invc_ref,
                     wqt_ref, wk_ref, wvt_ref, gqc_ref, gkn_ref, gkr_ref,
                     qt_ref, k_ref, vt_ref, gate_ref, *, q_scale):
    tm = x_ref.shape[0]
    h = _rms_rows(x_ref[...], g_ref[...]).astype(BF16)
    lat = _dot(h, wl_ref[...])
    cq = _rms_rows(lat[:, :A_LORA], glq_ref[...]).astype(BF16)
    ckv = _rms_rows(lat[:, A_LORA:2 * A_LORA], glkv_ref[...]).astype(BF16)
    kr = lat[:, 2 * A_LORA:]

    ang_t = invc_ref[...] * posr_ref[0]
    cos_t = jnp.cos(ang_t)
    sin_t = jnp.sin(ang_t)
    cos_c = cos_t[None]
    sin_c = sin_t[None]
    gqc = gqc_ref[...][None]
    hpc = 4
    n_chunk = A_HEADS // hpc
    g_cols = gate_ref.shape[1] // n_chunk
    half = A_ROPE // 2
    for c in range(n_chunk):
        rows = slice(c * hpc * A_QK, (c + 1) * hpc * A_QK)
        qt = _dot_nt(wqt_ref[rows, :], cq).reshape(hpc, A_QK, tm)
        ss = jnp.sum(qt * qt, axis=1, keepdims=True)
        qn = qt * (lax.rsqrt(ss * (1.0 / A_QK) + EPS) * q_scale) * gqc
        x1 = qn[:, A_NOPE:A_NOPE + half]
        x2 = qn[:, A_NOPE + half:]
        full = jnp.concatenate(
            [qn[:, :A_NOPE], x1 * cos_c - x2 * sin_c, x2 * cos_c + x1 * sin_c,
             jnp.zeros((hpc, A_QK_PAD - A_QK, tm), F32)], axis=1)
        qt_ref[0, c * hpc:(c + 1) * hpc, 0] = full.astype(BF16)
        sl = slice(c * g_cols, (c + 1) * g_cols)
        gate_ref[:, sl] = _silu(_dot(h, wg_ref[:, sl])).astype(BF16)

    ss_r = jnp.sum(kr * kr, axis=-1, keepdims=True)
    krt = (kr * gkr_ref[...]).T
    y1 = krt[:half]
    y2 = krt[half:A_ROPE]
    kr_rot = jnp.concatenate(
        [y1 * cos_t - y2 * sin_t, y2 * cos_t + y1 * sin_t, krt[A_ROPE:]], axis=0).T
    gkn = gkn_ref[...]
    kn_all = _dot(ckv, wk_ref[...])
    for hh in range(A_HEADS):
        kn = kn_all[:, hh * A_NOPE:(hh + 1) * A_NOPE]
        ss = jnp.sum(kn * kn, axis=-1, keepdims=True) + ss_r
        r = lax.rsqrt(ss * (1.0 / A_QK) + EPS)
        k_ref[0, hh, :, :A_NOPE] = (kn * gkn * r).astype(BF16)
        k_ref[0, hh, :, A_NOPE:] = (kr_rot * r).astype(BF16)

    for c in range(n_chunk):
        rows = slice(c * hpc * A_V, (c + 1) * hpc * A_V)
        vt = _dot_nt(wvt_ref[rows, :], ckv)
        vt_ref[0, c * hpc:(c + 1) * hpc, 0, :A_V, :] = vt.reshape(hpc, A_V, tm).astype(BF16)
    ones_row = lax.broadcasted_iota(jnp.int32, (A_HEADS, A_V_PAD - A_V, tm), 1) == 0
    vt_ref[0, :, 0, A_V:, :] = jnp.where(ones_row, 1.0, 0.0).astype(BF16)


def _mla_attn_kernel(qt_ref, k_ref, vt_ref, gate_ref, o_ref,
                     s0_ref, s1_ref, acc_ref, m_ref, *, tq, tk, tc):
    seq = k_ref.shape[2]
    tv = vt_ref.shape[4]
    assert tq == 2 * tk and qt_ref.shape[4] == tc and tk % tv == 0
    n_c = tq // tc

    def scores(s_ref, ks, qs, c):
        cs = slice(c * tc, (c + 1) * tc)
        s_ref[:, cs] = _dot(k_ref[0, 0, pl.ds(ks, tk), :], qt_ref[0, 0, qs // tc + c])

    def softmax_pv(s_ref, ks, c, mask):
        cs = slice(c * tc, (c + 1) * tc)
        st = s_ref[:, cs]
        if mask is not None:
            st = jnp.where(mask, st, NEG)
        m = m_ref[:, cs]
        m_new = jnp.maximum(m, jnp.max(st, axis=0, keepdims=True))
        p = jnp.exp2(st - m_new).astype(BF16)
        alpha = jnp.exp2(m - m_new)
        m_ref[:, cs] = m_new
        vt = jnp.concatenate([vt_ref[0, 0, ks // tv + i] for i in range(tk // tv)], axis=1)
        acc_ref[:, cs] = alpha * acc_ref[:, cs] + _dot(vt, p)

    def q_body(qi, _):
        qs = pl.multiple_of(qi * tq, tq)

        def pair(j, _):
            ks = pl.multiple_of(j * tq, tq)
            for c in range(n_c):
                scores(s1_ref, ks + tk, qs, c)
                softmax_pv(s0_ref, ks, c, None)
            for c in range(n_c):
                scores(s0_ref, ks + tq, qs, c)
                softmax_pv(s1_ref, ks + tk, c, None)
            return 0

        def finish(c):
            cs = slice(c * tc, (c + 1) * tc)
            o = (acc_ref[:A_V, cs] * (1.0 / acc_ref[A_V:A_V + 1, cs])).T
            rows = pl.ds(qs + c * tc, tc)
            o_ref[rows, :] = (o * gate_ref[rows, :].astype(F32)).astype(BF16)

        acc_ref[...] = jnp.zeros_like(acc_ref)
        m_ref[...] = jnp.full_like(m_ref, NEG)
        one = qi & 1
        two = lax.shift_right_logical(qi, 1) & 1

        @pl.when(one == 1)
        def _():
            pair(0, 0)

        @pl.when(two == 1)
        def _():
            pair(one, 0)
            pair(one + 1, 0)

        def four_pairs(j4, _):
            base = one + 2 * two + 4 * j4
            for i in range(4):
                pair(base + i, 0)
            return 0

        lax.fori_loop(0, lax.shift_right_logical(qi, 2), four_pairs, 0)
        row = lax.broadcasted_iota(jnp.int32, (tk, tc), 0)
        col = lax.broadcasted_iota(jnp.int32, (tk, tc), 1)
        for c in range(n_c):
            if (c + 1) * tc > tk:
                scores(s1_ref, qs + tk, qs, c)
        qs_next = pl.multiple_of(jnp.minimum(qs + tq, seq - tq), tq)
        for c in range(n_c):
            softmax_pv(s0_ref, qs, c, None if c * tc >= tk else row <= col + c * tc)
            if (c + 1) * tc <= tk:
                finish(c)
            scores(s0_ref, 0, qs_next, c)
        for c in range(n_c):
            if (c + 1) * tc > tk:
                softmax_pv(s1_ref, qs + tk, c, row + tk <= col + c * tc)
                finish(c)
        return 0

    for c in range(n_c):
        scores(s0_ref, 0, 0, c)
    lax.fori_loop(0, seq // tq, q_body, 0)


def _mla_out_kernel(og_ref, x_ref, wo_ref, gkv_ref, wkt_ref, wvt_ref, pos_ref, inv_ref, gk_ref,
                    x1_ref, k_ref, vt_ref):
    tm = x_ref.shape[0]
    x1 = x_ref[...] + _dot(og_ref[...], wo_ref[...])
    x1_ref[...] = x1
    hk = _rms_rows(x1, gkv_ref[...]).astype(BF16)
    kt = _dot_nt(wkt_ref[...], hk).reshape(B_KV_HEADS, B_HD, tm)
    ss = jnp.sum(kt * kt, axis=1, keepdims=True)
    kn = kt * lax.rsqrt(ss * (1.0 / B_HD) + EPS) * gk_ref[...][None]
    ang = inv_ref[...] * pos_ref[0]
    c = jnp.cos(ang)[None]
    s = jnp.sin(ang)[None]
    x1r = kn[:, :B_ROPE_HALF]
    x2r = kn[:, B_ROPE_HALF:2 * B_ROPE_HALF]
    kfull = jnp.concatenate(
        [x1r * c - x2r * s, x2r * c + x1r * s, kn[:, 2 * B_ROPE_HALF:],
         jnp.zeros((B_KV_HEADS, B_K_PAD - B_HD, tm), F32)], axis=1)
    k_ref[0] = kfull.reshape(B_KV_HEADS * B_K_PAD, tm).T.astype(BF16)
    vt = _dot_nt(wvt_ref[...], hk).reshape(B_KV_HEADS, B_HD, tm)
    vt_ref[0, :, :B_HD, :] = vt.astype(BF16)
    ones_row = lax.broadcasted_iota(jnp.int32, (B_KV_HEADS, B_V_PAD - B_HD, tm), 1) == 0
    vt_ref[0, :, B_HD:, :] = jnp.where(ones_row, 1.0, 0.0).astype(BF16)


def _swa_attn_kernel(sink_ref, x1_ref, g_ref, wqt_ref, wg_ref, pos_ref, inv_ref, gq_ref,
                     kc_ref, kp_ref, vc_ref, vp_ref, o_ref,
                     hb_ref, qt_ref, gate_ref, ot_ref, *, q_scale):
    w = B_WINDOW
    tm = x1_ref.shape[0]
    n_blk = tm // w
    j = pl.program_id(1)
    gw = B_GROUP * w
    hb_ref[...] = _rms_rows(x1_ref[...], g_ref[...]).astype(BF16)

    ang = inv_ref[...] * pos_ref[0]
    cos_c = jnp.cos(ang)[None]
    sin_c = jnp.sin(ang)[None]
    gq = gq_ref[...][None]
    hpp = 8
    for p in range(B_HEADS // hpp):
        rows = slice(p * hpp * B_HD, (p + 1) * hpp * B_HD)
        qt = _dot_nt(wqt_ref[rows, :], hb_ref[...]).reshape(hpp, B_HD, tm)
        ss = jnp.sum(qt * qt, axis=1, keepdims=True)
        qn = qt * (lax.rsqrt(ss * (1.0 / B_HD) + EPS) * q_scale) * gq
        x1r = qn[:, :B_ROPE_HALF]
        x2r = qn[:, B_ROPE_HALF:2 * B_ROPE_HALF]
        qfull = jnp.concatenate(
            [x1r * cos_c - x2r * sin_c, x2r * cos_c + x1r * sin_c, qn[:, 2 * B_ROPE_HALF:]], axis=1)
        qt_ref[rows, :] = qfull.reshape(hpp * B_HD, tm).astype(BF16)

    g_rows = 2 * w
    g_cols = 256
    gate_pieces = [(r, c) for r in range(tm // g_rows) for c in range(gate_ref.shape[1] // g_cols)]

    def gate_piece(r, c):
        rs = slice(r * g_rows, (r + 1) * g_rows)
        cs = slice(c * g_cols, (c + 1) * g_cols)
        gate_ref[rs, cs] = _silu(_dot(hb_ref[rs, :], wg_ref[:, cs])).astype(BF16)

    row = lax.broadcasted_iota(jnp.int32, (w, gw), 0)
    col = lax.broadcasted_iota(jnp.int32, (w, gw), 1)
    cur_visible = row <= (col & (w - 1))
    cur_visible_f = cur_visible.astype(F32)
    head_in_group = lax.broadcasted_iota(jnp.int32, (1, gw), 1) // w
    has_prev = j > 0

    def band_scores(blk, g):
        qs = slice(blk * w, (blk + 1) * w)
        ksl = slice(g * B_K_PAD, g * B_K_PAD + B_HD)
        if blk == 0:
            k_prev = kp_ref[0, :, ksl]
        else:
            k_prev = kc_ref[0, (blk - 1) * w:blk * w, ksl]
        k_band = jnp.concatenate([k_prev, kc_ref[0, qs, ksl]], axis=0)
        qt_g = jnp.concatenate(
            [qt_ref[(B_GROUP * g + i) * B_HD:(B_GROUP * g + i + 1) * B_HD, qs]
             for i in range(B_GROUP)], axis=1)
        return _dot(k_band, qt_g)

    def softmax_pv(st, blk, g):
        qs = slice(blk * w, (blk + 1) * w)
        s_prev = st[:w]
        if blk == 0:
            s_prev = jnp.where(has_prev, s_prev, NEG)
        s = jnp.where(cur_visible, st[w:], s_prev)
        sink = jnp.full((1, gw), sink_ref[B_GROUP * g] * LOG2E, F32)
        for i in range(1, B_GROUP):
            sink = jnp.where(head_in_group == i, sink_ref[B_GROUP * g + i] * LOG2E, sink)
        m = jnp.maximum(jnp.max(s, axis=0, keepdims=True), sink)
        e = jnp.exp2(s - m)
        e_cur = e * cur_visible_f
        p = jnp.concatenate([e - e_cur, e_cur], axis=0).astype(BF16)
        if blk == 0:
            v_prev = vp_ref[0, g]
        else:
            v_prev = vc_ref[0, g, :, (blk - 1) * w:blk * w]
        v_band = jnp.concatenate([v_prev, vc_ref[0, g, :, qs]], axis=1)
        pv = _dot(v_band, p)
        den = pv[B_HD:B_HD + 1] + jnp.exp2(sink - m)
        ot = pv[:B_HD] * (1.0 / den)
        for i in range(B_GROUP):
            hh = B_GROUP * g + i
            ot_ref[hh * B_HD:(hh + 1) * B_HD, qs] = ot[:, i * w:(i + 1) * w]

    def finish(blk):
        qs = slice(blk * w, (blk + 1) * w)
        o_ref[qs, :] = (ot_ref[:, qs].T * gate_ref[qs, :].astype(F32)).astype(BF16)

    order = [(blk, g) for blk in range(n_blk) for g in range(B_KV_HEADS)]
    assert len(order) == 2 * len(gate_pieces) and g_rows == 2 * w
    st = band_scores(*order[0])
    for idx, (blk, g) in enumerate(order):
        st_next = band_scores(*order[idx + 1]) if idx + 1 < len(order) else None
        softmax_pv(st, blk, g)
        if idx % 2 == 1:
            gate_piece(*gate_pieces[idx // 2])
        if g == B_KV_HEADS - 1 and blk % 2 == 1:
            finish(blk - 1)
            finish(blk)
        st = st_next


def _swa_out_kernel(og_ref, x1_ref, wo_ref, o_ref):
    o_ref[...] = x1_ref[...] + _dot(og_ref[...], wo_ref[...])


def kernel(x, positions, a_norm, a_w_in, a_q_latent_norm, a_w_q_up, a_kv_latent_norm, a_w_kv_up,
           a_q_head_norm, a_k_head_norm, a_w_o, kv_norm, kv_w, kv_k_head_norm, b_norm, b_w_in,
           b_q_head_norm, b_sinks, b_w_o):
    bsz, seq, d = x.shape
    t = bsz * seq
    assert a_norm.shape[0] == 1 and b_norm.shape[0] == 1
    assert seq % TQ_A == 0 and seq % QB_SWA == 0 and TQ_A % TK_A == 0

    xf = x.reshape(t, d)
    pos_row = positions.reshape(bsz, 1, seq).astype(F32)

    w_in = a_w_in[0]
    n_lat = 2 * A_LORA + A_ROPE
    wl = jnp.pad(w_in[:, :n_lat], ((0, 0), (0, 128 - A_ROPE))).astype(BF16)
    wg = w_in[:, n_lat:].astype(BF16)
    wqt = a_w_q_up[0].astype(BF16).T
    wkv = a_w_kv_up[0].astype(BF16).reshape(A_LORA, A_HEADS, A_NOPE + A_V)
    wk = wkv[:, :, :A_NOPE].reshape(A_LORA, A_HEADS * A_NOPE)
    wvt = wkv[:, :, A_NOPE:].reshape(A_LORA, A_HEADS * A_V).T
    gq_col = jnp.broadcast_to(a_q_head_norm[0][:, None], (A_QK, TM_UP))
    gkn = a_k_head_norm[0][:A_NOPE].reshape(1, A_NOPE)
    gkr = jnp.pad(a_k_head_norm[0][A_NOPE:], (0, 128 - A_ROPE)).reshape(1, 128)
    inv_a = jnp.power(jnp.float32(ROPE_THETA), -jnp.arange(0, A_ROPE, 2, dtype=F32) / A_ROPE)
    inv_a_col = jnp.broadcast_to(inv_a[:, None], (A_ROPE // 2, TM_UP))
    wo_a = a_w_o[0].astype(BF16)

    kv_dim = B_KV_HEADS * B_HD
    kv_w16 = kv_w.astype(BF16)
    wkt_b = kv_w16[:, :kv_dim].T
    wvt_b = kv_w16[:, kv_dim:].T
    rot_b = 2 * B_ROPE_HALF
    inv_b = jnp.power(jnp.float32(ROPE_THETA), -jnp.arange(0, rot_b, 2, dtype=F32) / rot_b)
    q_dim = B_HEADS * B_HD
    b_w_in16 = b_w_in[0].astype(BF16)
    wqt_b = b_w_in16[:, :q_dim].T
    wg_b = b_w_in16[:, q_dim:]
    wo_b = b_w_o[0].astype(BF16)

    ns_up = seq // TM_UP
    tok_up = lambda b, j: (b * ns_up + j, 0)
    qt_a, k_a, vt_a, gate_a = pl.pallas_call(
        functools.partial(_mla_proj_kernel, q_scale=A_QK ** -0.5 * LOG2E),
        grid=(bsz, ns_up),
        in_specs=[
            pl.BlockSpec((TM_UP, d), tok_up),
            _const_spec((1, d)),
            _const_spec(wl.shape),
            _const_spec(wg.shape),
            _const_spec((1, A_LORA)),
            _const_spec((1, A_LORA)),
            pl.BlockSpec((1, 1, TM_UP), lambda b, j: (b, 0, j)),
            _const_spec(inv_a_col.shape),
            _const_spec(wqt.shape),
            _const_spec(wk.shape),
            _const_spec(wvt.shape),
            _const_spec(gq_col.shape),
            _const_spec((1, A_NOPE)),
            _const_spec((1, 128)),
        ],
        out_specs=[
            pl.BlockSpec((1, A_HEADS, 1, A_QK_PAD, TM_UP), lambda b, j: (b, 0, j, 0, 0)),
            pl.BlockSpec((1, A_HEADS, TM_UP, A_QK_PAD), lambda b, j: (b, 0, j, 0)),
            pl.BlockSpec((1, A_HEADS, 1, A_V_PAD, TM_UP), lambda b, j: (b, 0, j, 0, 0)),
            pl.BlockSpec((TM_UP, d), tok_up),
        ],
        out_shape=[
            jax.ShapeDtypeStruct((bsz, A_HEADS, ns_up, A_QK_PAD, TM_UP), BF16),
            jax.ShapeDtypeStruct((bsz, A_HEADS, seq, A_QK_PAD), BF16),
            jax.ShapeDtypeStruct((bsz, A_HEADS, ns_up, A_V_PAD, TM_UP), BF16),
            jax.ShapeDtypeStruct((t, A_HEADS * A_V), BF16),
        ],
        compiler_params=_params(2),
        name="mla_proj",
    )(xf, a_norm[0].reshape(1, d), wl, wg, a_q_latent_norm[0].reshape(1, A_LORA),
      a_kv_latent_norm[0].reshape(1, A_LORA), pos_row, inv_a_col, wqt, wk, wvt, gq_col, gkn, gkr)

    og_a = pl.pallas_call(
        functools.partial(_mla_attn_kernel, tq=TQ_A, tk=TK_A, tc=TC_A),
        grid=(bsz, A_HEADS),
        in_specs=[
            pl.BlockSpec((1, 1, ns_up, A_QK_PAD, TM_UP), lambda b, h: (b, h, 0, 0, 0)),
            pl.BlockSpec((1, 1, seq, A_QK_PAD), lambda b, h: (b, h, 0, 0)),
            pl.BlockSpec((1, 1, ns_up, A_V_PAD, TM_UP), lambda b, h: (b, h, 0, 0, 0)),
            pl.BlockSpec((seq, A_V), lambda b, h: (b, h)),
        ],
        out_specs=pl.BlockSpec((seq, A_V), lambda b, h: (b, h)),
        out_shape=jax.ShapeDtypeStruct((t, A_HEADS * A_V), BF16),
        scratch_shapes=[pltpu.VMEM((TK_A, TQ_A), F32), pltpu.VMEM((TK_A, TQ_A), F32),
                        pltpu.VMEM((A_V_PAD, TQ_A), F32), pltpu.VMEM((1, TQ_A), F32)],
        compiler_params=_params(2),
        name="mla_attn",
    )(qt_a, k_a, vt_a, gate_a)

    ns_out = seq // TM_OUT
    tok_out = lambda b, j: (b * ns_out + j, 0)
    x1, k_b, vt_b = pl.pallas_call(
        _mla_out_kernel,
        grid=(bsz, ns_out),
        in_specs=[
            pl.BlockSpec((TM_OUT, d), tok_out),
            pl.BlockSpec((TM_OUT, d), tok_out),
            _const_spec(wo_a.shape),
            _const_spec((1, d)),
            _const_spec(wkt_b.shape),
            _const_spec(wvt_b.shape),
            pl.BlockSpec((1, 1, TM_OUT), lambda b, j: (b, 0, j)),
            _const_spec((B_ROPE_HALF, TM_OUT)),
            _const_spec((B_HD, TM_OUT)),
        ],
        out_specs=[
            pl.BlockSpec((TM_OUT, d), tok_out),
            pl.BlockSpec((1, TM_OUT, B_KV_HEADS * B_K_PAD), lambda b, j: (b, j, 0)),
            pl.BlockSpec((1, B_KV_HEADS, B_V_PAD, TM_OUT), lambda b, j: (b, 0, 0, j)),
        ],
        out_shape=[
            jax.ShapeDtypeStruct((t, d), F32),
            jax.ShapeDtypeStruct((bsz, seq, B_KV_HEADS * B_K_PAD), BF16),
            jax.ShapeDtypeStruct((bsz, B_KV_HEADS, B_V_PAD, seq), BF16),
        ],
        compiler_params=_params(2),
        name="mla_out",
    )(og_a, xf, wo_a, kv_norm.reshape(1, d), wkt_b, wvt_b, pos_row,
      jnp.broadcast_to(inv_b[:, None], (B_ROPE_HALF, TM_OUT)),
      jnp.broadcast_to(kv_k_head_norm[:, None], (B_HD, TM_OUT)))

    ns_sw = seq // QB_SWA
    blk_per_step = QB_SWA // B_WINDOW
    prev_blk = lambda b, j: jnp.maximum(j * blk_per_step - 1, 0)
    og_b = pl.pallas_call(
        functools.partial(_swa_attn_kernel, q_scale=B_HD ** -0.5 * LOG2E),
        grid=(bsz, ns_sw),
        in_specs=[
            pl.BlockSpec(memory_space=pltpu.SMEM),
            pl.BlockSpec((QB_SWA, d), lambda b, j: (b * ns_sw + j, 0)),
            _const_spec((1, d)),
            _const_spec(wqt_b.shape),
            _const_spec(wg_b.shape),
            pl.BlockSpec((1, 1, QB_SWA), lambda b, j: (b, 0, j)),
            _const_spec((B_ROPE_HALF, QB_SWA)),
            _const_spec((B_HD, QB_SWA)),
            pl.BlockSpec((1, QB_SWA, B_KV_HEADS * B_K_PAD), lambda b, j: (b, j, 0)),
            pl.BlockSpec((1, B_WINDOW, B_KV_HEADS * B_K_PAD), lambda b, j: (b, prev_blk(b, j), 0)),
            pl.BlockSpec((1, B_KV_HEADS, B_V_PAD, QB_SWA), lambda b, j: (b, 0, 0, j)),
            pl.BlockSpec((1, B_KV_HEADS, B_V_PAD, B_WINDOW), lambda b, j: (b, 0, 0, prev_blk(b, j))),
        ],
        out_specs=pl.BlockSpec((QB_SWA, q_dim), lambda b, j: (b * ns_sw + j, 0)),
        out_shape=jax.ShapeDtypeStruct((t, q_dim), BF16),
        scratch_shapes=[pltpu.VMEM((QB_SWA, d), BF16), pltpu.VMEM((q_dim, QB_SWA), BF16),
                        pltpu.VMEM((QB_SWA, q_dim), BF16), pltpu.VMEM((q_dim, QB_SWA), F32)],
        compiler_params=_params(2),
        name="swa_attn",
    )(b_sinks[0], x1, b_norm[0].reshape(1, d), wqt_b, wg_b, pos_row,
      jnp.broadcast_to(inv_b[:, None], (B_ROPE_HALF, QB_SWA)),
      jnp.broadcast_to(b_q_head_norm[0][:, None], (B_HD, QB_SWA)),
      k_b, k_b, vt_b, vt_b)

    n_out = t // TM_OUT
    out = pl.pallas_call(
        _swa_out_kernel,
        grid=(n_out,),
        in_specs=[
            pl.BlockSpec((TM_OUT, q_dim), lambda i: (i, 0)),
            pl.BlockSpec((TM_OUT, d), lambda i: (i, 0)),
            _const_spec(wo_b.shape),
        ],
        out_specs=pl.BlockSpec((TM_OUT, d), lambda i: (i, 0)),
        out_shape=jax.ShapeDtypeStruct((t, d), F32),
        compiler_params=_params(1),
        name="swa_out",
    )(og_b, x1, wo_b)
    return out.reshape(bsz, seq, d)
```

```python
import functools
import math

import jax
import jax.numpy as jnp
from jax import lax
from jax.experimental import pallas as pl
from jax.experimental.pallas import tpu as pltpu

F32 = jnp.float32
BF16 = jnp.bfloat16

EPS = 1e-6
ROPE_THETA = 500000.0
NEG = -1e30
LOG2E = math.log2(math.e)

A_HEADS = 16
A_LORA = 512
A_NOPE = 128
A_ROPE = 64
A_QK = A_NOPE + A_ROPE
A_QK_PAD = 256
A_V = 128
A_V_PAD = A_V + 16

B_HEADS = 32
B_KV_HEADS = 8
B_GROUP = B_HEADS // B_KV_HEADS
B_HD = 64
B_ROPE_HALF = 8
B_WINDOW = 128
B_K_PAD = 128
B_V_PAD = B_HD + 16

VMEM_LIMIT = 56 * 1024 * 1024

TM_UP = 256
TQ_A = 1024
TK_A = 512
TC_A = 256
TM_OUT = 512
QB_SWA = 512


def _const_spec(shape):
    return pl.BlockSpec(shape, lambda *_: (0,) * len(shape), pipeline_mode=pl.Buffered(1))


def _params(n_axes):
    return pltpu.CompilerParams(
        dimension_semantics=("arbitrary",) * n_axes, vmem_limit_bytes=VMEM_LIMIT)


def _rms_rows(xf, g_row):
    ms = jnp.mean(xf * xf, axis=-1, keepdims=True)
    return xf * lax.rsqrt(ms + EPS) * g_row


def _dot(a, b):
    return jnp.dot(a, b, preferred_element_type=F32)


def _dot_nt(a, b):
    return lax.dot_general(a, b, (((1,), (1,)), ((), ())), preferred_element_type=F32)


def _silu(g):
    return g * jax.nn.sigmoid(g)


def _mla_proj_kernel(x_ref, g_ref, wl_ref, wg_ref, glq_ref, glkv_ref, posr_ref, invc_ref,
                     wqt_ref, wk_ref, wvt_ref, gqc_ref, gkn_ref, gkr_ref,
                     qt_ref, k_ref, vt_ref, gate_ref, *, q_scale):
    tm = x_ref.shape[0]
    h = _rms_rows(x_ref[...], g_ref[...]).astype(BF16)
    lat = _dot(h, wl_ref[...])
    cq = _rms_rows(lat[:, :A_LORA], glq_ref[...]).astype(BF16)
    ckv = _rms_rows(lat[:, A_LORA:2 * A_LORA], glkv_ref[...]).astype(BF16)
    kr = lat[:, 2 * A_LORA:]

    ang_t = invc_ref[...] * posr_ref[0]
    cos_t = jnp.cos(ang_t)
    sin_t = jnp.sin(ang_t)
    cos_c = cos_t[None]
    sin_c = sin_t[None]
    gqc = gqc_ref[...][None]
    hpc = 4
    n_chunk = A_HEADS // hpc
    g_cols = gate_ref.shape[1] // n_chunk
    half = A_ROPE // 2
    for c in range(n_chunk):
        rows = slice(c * hpc * A_QK, (c + 1) * hpc * A_QK)
        qt = _dot_nt(wqt_ref[rows, :], cq).reshape(hpc, A_QK, tm)
        ss = jnp.sum(qt * qt, axis=1, keepdims=True)
        qn = qt * (lax.rsqrt(ss * (1.0 / A_QK) + EPS) * q_scale) * gqc
        x1 = qn[:, A_NOPE:A_NOPE + half]
        x2 = qn[:, A_NOPE + half:]
        full = jnp.concatenate(
            [qn[:, :A_NOPE], x1 * cos_c - x2 * sin_c, x2 * cos_c + x1 * sin_c,
             jnp.zeros((hpc, A_QK_PAD - A_QK, tm), F32)], axis=1)
        qt_ref[0, c * hpc:(c + 1) * hpc, 0] = full.astype(BF16)
        sl = slice(c * g_cols, (c + 1) * g_cols)
        gate_ref[:, sl] = _silu(_dot(h, wg_ref[:, sl])).astype(BF16)

    ss_r = jnp.sum(kr * kr, axis=-1, keepdims=True)
    krt = (kr * gkr_ref[...]).T
    y1 = krt[:half]
    y2 = krt[half:A_ROPE]
    kr_rot = jnp.concatenate(
        [y1 * cos_t - y2 * sin_t, y2 * cos_t + y1 * sin_t, krt[A_ROPE:]], axis=0).T
    gkn = gkn_ref[...]
    kn_all = _dot(ckv, wk_ref[...])
    for hh in range(A_HEADS):
        kn = kn_all[:, hh * A_NOPE:(hh + 1) * A_NOPE]
        ss = jnp.sum(kn * kn, axis=-1, keepdims=True) + ss_r
        r = lax.rsqrt(ss * (1.0 / A_QK) + EPS)
        k_ref[0, hh, :, :A_NOPE] = (kn * gkn * r).astype(BF16)
        k_ref[0, hh, :, A_NOPE:] = (kr_rot * r).astype(BF16)

    for c in range(n_chunk):
        rows = slice(c * hpc * A_V, (c + 1) * hpc * A_V)
        vt = _dot_nt(wvt_ref[rows, :], ckv)
        vt_ref[0, c * hpc:(c + 1) * hpc, 0, :A_V, :] = vt.reshape(hpc, A_V, tm).astype(BF16)
    ones_row = lax.broadcasted_iota(jnp.int32, (A_HEADS, A_V_PAD - A_V, tm), 1) == 0
    vt_ref[0, :, 0, A_V:, :] = jnp.where(ones_row, 1.0, 0.0).astype(BF16)


def _mla_attn_kernel(qt_ref, k_ref, vt_ref, gate_ref, o_ref,
                     s0_ref, s1_ref, acc_ref, m_ref, *, tq, tk, tc):
    seq = k_ref.shape[2]
    tv = vt_ref.shape[4]
    assert tq == 2 * tk and qt_ref.shape[4] == tc and tk % tv == 0
    n_c = tq // tc

    def scores(s_ref, ks, qs, c):
        s_ref[c] = _dot(k_ref[0, 0, pl.ds(ks, tk), :], qt_ref[0, 0, qs // tc + c])

    def softmax_pv(s_ref, ks, c, mask):
        st = s_ref[c]
        if mask is not None:
            st = jnp.where(mask, st, NEG)
        m = m_ref[c]
        m_new = jnp.maximum(m, jnp.max(st, axis=0, keepdims=True))
        p = jnp.exp2(st - m_new).astype(BF16)
        alpha = jnp.exp2(m - m_new)
        m_ref[c] = m_new
        vt = jnp.concatenate([vt_ref[0, 0, ks // tv + i] for i in range(tk // tv)], axis=1)
        acc_ref[c] = alpha * acc_ref[c] + _dot(vt, p)

    def q_body(qi, _):
        qs = pl.multiple_of(qi * tq, tq)

        def pair(j, _):
            ks = pl.multiple_of(j * tq, tq)
            for c in range(n_c):
                scores(s1_ref, ks + tk, qs, c)
                softmax_pv(s0_ref, ks, c, None)
            for c in range(n_c):
                scores(s0_ref, ks + tq, qs, c)
                softmax_pv(s1_ref, ks + tk, c, None)
            return 0

        def finish(c):
            o = (acc_ref[c, :A_V] * (1.0 / acc_ref[c, A_V:A_V + 1])).T
            rows = pl.ds(qs + c * tc, tc)
            o_ref[rows, :] = (o * gate_ref[rows, :].astype(F32)).astype(BF16)

        acc_ref[...] = jnp.zeros_like(acc_ref)
        m_ref[...] = jnp.full_like(m_ref, NEG)
        one = qi & 1
        two = lax.shift_right_logical(qi, 1) & 1

        @pl.when(one == 1)
        def _():
            pair(0, 0)

        @pl.when(two == 1)
        def _():
            pair(one, 0)
            pair(one + 1, 0)

        def four_pairs(j4, _):
            base = one + 2 * two + 4 * j4
            for i in range(4):
                pair(base + i, 0)
            return 0

        lax.fori_loop(0, lax.shift_right_logical(qi, 2), four_pairs, 0)
        row = lax.broadcasted_iota(jnp.int32, (tk, tc), 0)
        col = lax.broadcasted_iota(jnp.int32, (tk, tc), 1)
        for c in range(n_c):
            if (c + 1) * tc > tk:
                scores(s1_ref, qs + tk, qs, c)
        qs_next = pl.multiple_of(jnp.minimum(qs + tq, seq - tq), tq)
        for c in range(n_c):
            softmax_pv(s0_ref, qs, c, None if c * tc >= tk else row <= col + c * tc)
            if (c + 1) * tc <= tk:
                finish(c)
            scores(s0_ref, 0, qs_next, c)
        for c in range(n_c):
            if (c + 1) * tc > tk:
                softmax_pv(s1_ref, qs + tk, c, row + tk <= col + c * tc)
                finish(c)
        return 0

    for c in range(n_c):
        scores(s0_ref, 0, 0, c)
    lax.fori_loop(0, seq // tq, q_body, 0)


def _mla_out_kernel(og_ref, x_ref, wo_ref, gkv_ref, wkt_ref, wvt_ref, pos_ref, inv_ref, gk_ref,
                    x1_ref, k_ref, vt_ref):
    tm = x_ref.shape[0]
    x1 = x_ref[...] + _dot(og_ref[...], wo_ref[...])
    x1_ref[...] = x1
    hk = _rms_rows(x1, gkv_ref[...]).astype(BF16)
    kt = _dot_nt(wkt_ref[...], hk).reshape(B_KV_HEADS, B_HD, tm)
    ss = jnp.sum(kt * kt, axis=1, keepdims=True)
    kn = kt * lax.rsqrt(ss * (1.0 / B_HD) + EPS) * gk_ref[...][None]
    ang = inv_ref[...] * pos_ref[0]
    c = jnp.cos(ang)[None]
    s = jnp.sin(ang)[None]
    x1r = kn[:, :B_ROPE_HALF]
    x2r = kn[:, B_ROPE_HALF:2 * B_ROPE_HALF]
    kfull = jnp.concatenate(
        [x1r * c - x2r * s, x2r * c + x1r * s, kn[:, 2 * B_ROPE_HALF:],
         jnp.zeros((B_KV_HEADS, B_K_PAD - B_HD, tm), F32)], axis=1)
    k_ref[0] = kfull.reshape(B_KV_HEADS * B_K_PAD, tm).T.astype(BF16)
    vt = _dot_nt(wvt_ref[...], hk).reshape(B_KV_HEADS, B_HD, tm)
    vt_ref[0, :, :B_HD, :] = vt.astype(BF16)
    ones_row = lax.broadcasted_iota(jnp.int32, (B_KV_HEADS, B_V_PAD - B_HD, tm), 1) == 0
    vt_ref[0, :, B_HD:, :] = jnp.where(ones_row, 1.0, 0.0).astype(BF16)


def _swa_attn_kernel(sink_ref, x1_ref, g_ref, wqt_ref, wg_ref, pos_ref, inv_ref, gq_ref,
                     kc_ref, kp_ref, vc_ref, vp_ref, o_ref,
                     hb_ref, qt_ref, gate_ref, ot_ref, *, q_scale):
    w = B_WINDOW
    tm = x1_ref.shape[0]
    n_blk = tm // w
    j = pl.program_id(1)
    gw = B_GROUP * w
    hb_ref[...] = _rms_rows(x1_ref[...], g_ref[...]).astype(BF16)

    ang = inv_ref[...] * pos_ref[0]
    cos_c = jnp.cos(ang)[None]
    sin_c = jnp.sin(ang)[None]
    gq = gq_ref[...][None]
    hpp = 8
    for p in range(B_HEADS // hpp):
        rows = slice(p * hpp * B_HD, (p + 1) * hpp * B_HD)
        qt = _dot_nt(wqt_ref[rows, :], hb_ref[...]).reshape(hpp, B_HD, tm)
        ss = jnp.sum(qt * qt, axis=1, keepdims=True)
        qn = qt * (lax.rsqrt(ss * (1.0 / B_HD) + EPS) * q_scale) * gq
        x1r = qn[:, :B_ROPE_HALF]
        x2r = qn[:, B_ROPE_HALF:2 * B_ROPE_HALF]
        qfull = jnp.concatenate(
            [x1r * cos_c - x2r * sin_c, x2r * cos_c + x1r * sin_c, qn[:, 2 * B_ROPE_HALF:]], axis=1)
        qb = qfull.reshape(hpp * B_HD, tm).astype(BF16)
        for blk in range(n_blk):
            qt_ref[blk, rows, :] = qb[:, blk * w:(blk + 1) * w]

    g_rows = 2 * w
    g_cols = 256
    gate_pieces = [(r, c) for r in range(tm // g_rows) for c in range(gate_ref.shape[1] // g_cols)]

    def gate_piece(r, c):
        rs = slice(r * g_rows, (r + 1) * g_rows)
        cs = slice(c * g_cols, (c + 1) * g_cols)
        gate_ref[rs, cs] = _silu(_dot(hb_ref[rs, :], wg_ref[:, cs])).astype(BF16)

    row = lax.broadcasted_iota(jnp.int32, (w, gw), 0)
    col = lax.broadcasted_iota(jnp.int32, (w, gw), 1)
    cur_visible = row <= (col & (w - 1))
    cur_visible_f = cur_visible.astype(F32)
    head_in_group = lax.broadcasted_iota(jnp.int32, (1, gw), 1) // w
    has_prev = j > 0

    def band_scores(blk, g):
        qs = slice(blk * w, (blk + 1) * w)
        ksl = slice(g * B_K_PAD, g * B_K_PAD + B_HD)
        if blk == 0:
            k_prev = kp_ref[0, :, ksl]
        else:
            k_prev = kc_ref[0, (blk - 1) * w:blk * w, ksl]
        k_band = jnp.concatenate([k_prev, kc_ref[0, qs, ksl]], axis=0)
        qt_g = jnp.concatenate(
            [qt_ref[blk, (B_GROUP * g + i) * B_HD:(B_GROUP * g + i + 1) * B_HD, :]
             for i in range(B_GROUP)], axis=1)
        return _dot(k_band, qt_g)

    def softmax_pv(st, blk, g):
        qs = slice(blk * w, (blk + 1) * w)
        s_prev = st[:w]
        if blk == 0:
            s_prev = jnp.where(has_prev, s_prev, NEG)
        s = jnp.where(cur_visible, st[w:], s_prev)
        sink = jnp.full((1, gw), sink_ref[B_GROUP * g] * LOG2E, F32)
        for i in range(1, B_GROUP):
            sink = jnp.where(head_in_group == i, sink_ref[B_GROUP * g + i] * LOG2E, sink)
        m = jnp.maximum(jnp.max(s, axis=0, keepdims=True), sink)
        e = jnp.exp2(s - m)
        e_cur = e * cur_visible_f
        p = jnp.concatenate([e - e_cur, e_cur], axis=0).astype(BF16)
        if blk == 0:
            v_prev = vp_ref[0, g]
        else:
            v_prev = vc_ref[0, g, :, (blk - 1) * w:blk * w]
        v_band = jnp.concatenate([v_prev, vc_ref[0, g, :, qs]], axis=1)
        pv = _dot(v_band, p)
        den = pv[B_HD:B_HD + 1] + jnp.exp2(sink - m)
        ot = pv[:B_HD] * (1.0 / den)
        for i in range(B_GROUP):
            hh = B_GROUP * g + i
            ot_ref[blk, hh * B_HD:(hh + 1) * B_HD, :] = ot[:, i * w:(i + 1) * w]

    def finish(blk):
        qs = slice(blk * w, (blk + 1) * w)
        o_ref[qs, :] = (ot_ref[blk].T * gate_ref[qs, :].astype(F32)).astype(BF16)

    order = [(blk, g) for blk in range(n_blk) for g in range(B_KV_HEADS)]
    assert len(order) == 2 * len(gate_pieces) and g_rows == 2 * w
    st = band_scores(*order[0])
    for idx, (blk, g) in enumerate(order):
        st_next = band_scores(*order[idx + 1]) if idx + 1 < len(order) else None
        softmax_pv(st, blk, g)
        if idx % 2 == 1:
            gate_piece(*gate_pieces[idx // 2])
        if g == B_KV_HEADS - 1 and blk % 2 == 1:
            finish(blk - 1)
            finish(blk)
        st = st_next


def _swa_out_kernel(og_ref, x1_ref, wo_ref, o_ref):
    o_ref[...] = x1_ref[...] + _dot(og_ref[...], wo_ref[...].astype(BF16))


def kernel(x, positions, a_norm, a_w_in, a_q_latent_norm, a_w_q_up, a_kv_latent_norm, a_w_kv_up,
           a_q_head_norm, a_k_head_norm, a_w_o, kv_norm, kv_w, kv_k_head_norm, b_norm, b_w_in,
           b_q_head_norm, b_sinks, b_w_o):
    bsz, seq, d = x.shape
    t = bsz * seq
    assert a_norm.shape[0] == 1 and b_norm.shape[0] == 1
    assert seq % TQ_A == 0 and seq % QB_SWA == 0 and TQ_A % TK_A == 0

    xf = x.reshape(t, d)
    pos_row = positions.reshape(bsz, 1, seq).astype(F32)

    w_in = a_w_in[0]
    n_lat = 2 * A_LORA + A_ROPE
    wl = jnp.pad(w_in[:, :n_lat], ((0, 0), (0, 128 - A_ROPE))).astype(BF16)
    wg = w_in[:, n_lat:].astype(BF16)
    wqt = a_w_q_up[0].astype(BF16).T
    wkv = a_w_kv_up[0].astype(BF16).reshape(A_LORA, A_HEADS, A_NOPE + A_V)
    wk = wkv[:, :, :A_NOPE].reshape(A_LORA, A_HEADS * A_NOPE)
    wvt = wkv[:, :, A_NOPE:].reshape(A_LORA, A_HEADS * A_V).T
    gq_col = jnp.broadcast_to(a_q_head_norm[0][:, None], (A_QK, TM_UP))
    gkn = a_k_head_norm[0][:A_NOPE].reshape(1, A_NOPE)
    gkr = jnp.pad(a_k_head_norm[0][A_NOPE:], (0, 128 - A_ROPE)).reshape(1, 128)
    inv_a = jnp.power(jnp.float32(ROPE_THETA), -jnp.arange(0, A_ROPE, 2, dtype=F32) / A_ROPE)
    inv_a_col = jnp.broadcast_to(inv_a[:, None], (A_ROPE // 2, TM_UP))
    wo_a = a_w_o[0].astype(BF16)

    kv_dim = B_KV_HEADS * B_HD
    kv_w16 = kv_w.astype(BF16)
    wkt_b = kv_w16[:, :kv_dim].T
    wvt_b = kv_w16[:, kv_dim:].T
    rot_b = 2 * B_ROPE_HALF
    inv_b = jnp.power(jnp.float32(ROPE_THETA), -jnp.arange(0, rot_b, 2, dtype=F32) / rot_b)
    q_dim = B_HEADS * B_HD
    b_w_in16 = b_w_in[0].astype(BF16)
    wqt_b = b_w_in16[:, :q_dim].T
    wg_b = b_w_in16[:, q_dim:]
    wo_b = b_w_o[0]

    ns_up = seq // TM_UP
    tok_up = lambda b, j: (b * ns_up + j, 0)
    qt_a, k_a, vt_a, gate_a = pl.pallas_call(
        functools.partial(_mla_proj_kernel, q_scale=A_QK ** -0.5 * LOG2E),
        grid=(bsz, ns_up),
        in_specs=[
            pl.BlockSpec((TM_UP, d), tok_up),
            _const_spec((1, d)),
            _const_spec(wl.shape),
            _const_spec(wg.shape),
            _const_spec((1, A_LORA)),
            _const_spec((1, A_LORA)),
            pl.BlockSpec((1, 1, TM_UP), lambda b, j: (b, 0, j)),
            _const_spec(inv_a_col.shape),
            _const_spec(wqt.shape),
            _const_spec(wk.shape),
            _const_spec(wvt.shape),
            _const_spec(gq_col.shape),
            _const_spec((1, A_NOPE)),
            _const_spec((1, 128)),
        ],
        out_specs=[
            pl.BlockSpec((1, A_HEADS, 1, A_QK_PAD, TM_UP), lambda b, j: (b, 0, j, 0, 0)),
            pl.BlockSpec((1, A_HEADS, TM_UP, A_QK_PAD), lambda b, j: (b, 0, j, 0)),
            pl.BlockSpec((1, A_HEADS, 1, A_V_PAD, TM_UP), lambda b, j: (b, 0, j, 0, 0)),
            pl.BlockSpec((TM_UP, d), tok_up),
        ],
        out_shape=[
            jax.ShapeDtypeStruct((bsz, A_HEADS, ns_up, A_QK_PAD, TM_UP), BF16),
            jax.ShapeDtypeStruct((bsz, A_HEADS, seq, A_QK_PAD), BF16),
            jax.ShapeDtypeStruct((bsz, A_HEADS, ns_up, A_V_PAD, TM_UP), BF16),
            jax.ShapeDtypeStruct((t, A_HEADS * A_V), BF16),
        ],
        compiler_params=_params(2),
        name="mla_proj",
    )(xf, a_norm[0].reshape(1, d), wl, wg, a_q_latent_norm[0].reshape(1, A_LORA),
      a_kv_latent_norm[0].reshape(1, A_LORA), pos_row, inv_a_col, wqt, wk, wvt, gq_col, gkn, gkr)

    og_a = pl.pallas_call(
        functools.partial(_mla_attn_kernel, tq=TQ_A, tk=TK_A, tc=TC_A),
        grid=(bsz, A_HEADS),
        in_specs=[
            pl.BlockSpec((1, 1, ns_up, A_QK_PAD, TM_UP), lambda b, h: (b, h, 0, 0, 0)),
            pl.BlockSpec((1, 1, seq, A_QK_PAD), lambda b, h: (b, h, 0, 0)),
            pl.BlockSpec((1, 1, ns_up, A_V_PAD, TM_UP), lambda b, h: (b, h, 0, 0, 0)),
            pl.BlockSpec((seq, A_V), lambda b, h: (b, h)),
        ],
        out_specs=pl.BlockSpec((seq, A_V), lambda b, h: (b, h)),
        out_shape=jax.ShapeDtypeStruct((t, A_HEADS * A_V), BF16),
        scratch_shapes=[pltpu.VMEM((TQ_A // TC_A, TK_A, TC_A), F32), pltpu.VMEM((TQ_A // TC_A, TK_A, TC_A), F32),
                        pltpu.VMEM((TQ_A // TC_A, A_V_PAD, TC_A), F32), pltpu.VMEM((TQ_A // TC_A, 1, TC_A), F32)],
        compiler_params=_params(2),
        name="mla_attn",
    )(qt_a, k_a, vt_a, gate_a)

    ns_out = seq // TM_OUT
    tok_out = lambda b, j: (b * ns_out + j, 0)
    x1, k_b, vt_b = pl.pallas_call(
        _mla_out_kernel,
        grid=(bsz, ns_out),
        in_specs=[
            pl.BlockSpec((TM_OUT, d), tok_out),
            pl.BlockSpec((TM_OUT, d), tok_out),
            _const_spec(wo_a.shape),
            _const_spec((1, d)),
            _const_spec(wkt_b.shape),
            _const_spec(wvt_b.shape),
            pl.BlockSpec((1, 1, TM_OUT), lambda b, j: (b, 0, j)),
            _const_spec((B_ROPE_HALF, TM_OUT)),
            _const_spec((B_HD, TM_OUT)),
        ],
        out_specs=[
            pl.BlockSpec((TM_OUT, d), tok_out),
            pl.BlockSpec((1, TM_OUT, B_KV_HEADS * B_K_PAD), lambda b, j: (b, j, 0)),
            pl.BlockSpec((1, B_KV_HEADS, B_V_PAD, TM_OUT), lambda b, j: (b, 0, 0, j)),
        ],
        out_shape=[
            jax.ShapeDtypeStruct((t, d), F32),
            jax.ShapeDtypeStruct((bsz, seq, B_KV_HEADS * B_K_PAD), BF16),
            jax.ShapeDtypeStruct((bsz, B_KV_HEADS, B_V_PAD, seq), BF16),
        ],
        compiler_params=_params(2),
        name="mla_out",
    )(og_a, xf, wo_a, kv_norm.reshape(1, d), wkt_b, wvt_b, pos_row,
      jnp.broadcast_to(inv_b[:, None], (B_ROPE_HALF, TM_OUT)),
      jnp.broadcast_to(kv_k_head_norm[:, None], (B_HD, TM_OUT)))

    ns_sw = seq // QB_SWA
    blk_per_step = QB_SWA // B_WINDOW
    prev_blk = lambda b, j: jnp.maximum(j * blk_per_step - 1, 0)
    og_b = pl.pallas_call(
        functools.partial(_swa_attn_kernel, q_scale=B_HD ** -0.5 * LOG2E),
        grid=(bsz, ns_sw),
        in_specs=[
            pl.BlockSpec(memory_space=pltpu.SMEM),
            pl.BlockSpec((QB_SWA, d), lambda b, j: (b * ns_sw + j, 0)),
            _const_spec((1, d)),
            _const_spec(wqt_b.shape),
            _const_spec(wg_b.shape),
            pl.BlockSpec((1, 1, QB_SWA), lambda b, j: (b, 0, j)),
            _const_spec((B_ROPE_HALF, QB_SWA)),
            _const_spec((B_HD, QB_SWA)),
            pl.BlockSpec((1, QB_SWA, B_KV_HEADS * B_K_PAD), lambda b, j: (b, j, 0)),
            pl.BlockSpec((1, B_WINDOW, B_KV_HEADS * B_K_PAD), lambda b, j: (b, prev_blk(b, j), 0)),
            pl.BlockSpec((1, B_KV_HEADS, B_V_PAD, QB_SWA), lambda b, j: (b, 0, 0, j)),
            pl.BlockSpec((1, B_KV_HEADS, B_V_PAD, B_WINDOW), lambda b, j: (b, 0, 0, prev_blk(b, j))),
        ],
        out_specs=pl.BlockSpec((QB_SWA, q_dim), lambda b, j: (b * ns_sw + j, 0)),
        out_shape=jax.ShapeDtypeStruct((t, q_dim), BF16),
        scratch_shapes=[pltpu.VMEM((QB_SWA, d), BF16),
                        pltpu.VMEM((blk_per_step, q_dim, B_WINDOW), BF16),
                        pltpu.VMEM((QB_SWA, q_dim), BF16),
                        pltpu.VMEM((blk_per_step, q_dim, B_WINDOW), F32)],
        compiler_params=_params(2),
        name="swa_attn",
    )(b_sinks[0], x1, b_norm[0].reshape(1, d), wqt_b, wg_b, pos_row,
      jnp.broadcast_to(inv_b[:, None], (B_ROPE_HALF, QB_SWA)),
      jnp.broadcast_to(b_q_head_norm[0][:, None], (B_HD, QB_SWA)),
      k_b, k_b, vt_b, vt_b)

    n_out = t // TM_OUT
    out = pl.pallas_call(
        _swa_out_kernel,
        grid=(n_out,),
        in_specs=[
            pl.BlockSpec((TM_OUT, q_dim), lambda i: (i, 0)),
            pl.BlockSpec((TM_OUT, d), lambda i: (i, 0)),
            _const_spec(wo_b.shape),
        ],
        out_specs=pl.BlockSpec((TM_OUT, d), lambda i: (i, 0)),
        out_shape=jax.ShapeDtypeStruct((t, d), F32),
        compiler_params=_params(1),
        name="swa_out",
    )(og_b, x1, wo_b)
    return out.reshape(bsz, seq, d)
```

```python
import functools
import math

import jax
import jax.numpy as jnp
from jax import lax
from jax.experimental import pallas as pl
from jax.experimental.pallas import tpu as pltpu

F32 = jnp.float32
BF16 = jnp.bfloat16

EPS = 1e-6
ROPE_THETA = 500000.0
NEG = -1e30
LOG2E = math.log2(math.e)

A_HEADS = 16
A_LORA = 512
A_NOPE = 128
A_ROPE = 64
A_QK = A_NOPE + A_ROPE
A_QK_PAD = 256
A_V = 128
A_V_PAD = A_V + 16

B_HEADS = 32
B_KV_HEADS = 8
B_GROUP = B_HEADS // B_KV_HEADS
B_HD = 64
B_ROPE_HALF = 8
B_WINDOW = 128
B_K_PAD = 128
B_V_PAD = B_HD + 16

VMEM_LIMIT = 56 * 1024 * 1024

TM_UP = 256
TQ_A = 1024
TK_A = 512
TC_A = 256
TM_OUT = 512
QB_SWA = 512


def _const_spec(shape):
    return pl.BlockSpec(shape, lambda *_: (0,) * len(shape), pipeline_mode=pl.Buffered(1))


def _params(n_axes):
    return pltpu.CompilerParams(
        dimension_semantics=("arbitrary",) * n_axes, vmem_limit_bytes=VMEM_LIMIT)


def _rms_rows(xf, g_row):
    ms = jnp.mean(xf * xf, axis=-1, keepdims=True)
    return xf * lax.rsqrt(ms + EPS) * g_row


def _dot(a, b):
    return jnp.dot(a, b, preferred_element_type=F32)


def _dot_nt(a, b):
    return lax.dot_general(a, b, (((1,), (1,)), ((), ())), preferred_element_type=F32)


def _silu(g):
    return g * jax.nn.sigmoid(g)


def _mla_proj_kernel(x_ref, g_ref, win_ref, glq_ref, glkv_ref, posr_ref, invc_ref,
                     wqt_ref, wk_ref, wvt_ref, gqc_ref, gkn_ref, gkr_ref,
                     qt_ref, k_ref, vt_ref, gate_ref, *, q_scale):
    tm = x_ref.shape[0]
    h = _rms_rows(x_ref[...], g_ref[...]).astype(BF16)
    n_gate = gate_ref.shape[1]
    lat = _dot(h, win_ref[:, n_gate:])
    cq = _rms_rows(lat[:, :A_LORA], glq_ref[...]).astype(BF16)
    ckv = _rms_rows(lat[:, A_LORA:2 * A_LORA], glkv_ref[...]).astype(BF16)
    kr = lat[:, 2 * A_LORA:]

    ang_t = invc_ref[...] * posr_ref[0]
    cos_t = jnp.cos(ang_t)
    sin_t = jnp.sin(ang_t)
    cos_c = cos_t[None]
    sin_c = sin_t[None]
    gqc = gqc_ref[...][None]
    hpc = 4
    n_chunk = A_HEADS // hpc
    g_cols = gate_ref.shape[1] // n_chunk
    half = A_ROPE // 2
    for c in range(n_chunk):
        rows = slice(c * hpc * A_QK, (c + 1) * hpc * A_QK)
        qt = _dot_nt(wqt_ref[rows, :], cq).reshape(hpc, A_QK, tm)
        ss = jnp.sum(qt * qt, axis=1, keepdims=True)
        qn = qt * (lax.rsqrt(ss * (1.0 / A_QK) + EPS) * q_scale) * gqc
        x1 = qn[:, A_NOPE:A_NOPE + half]
        x2 = qn[:, A_NOPE + half:]
        full = jnp.concatenate(
            [qn[:, :A_NOPE], x1 * cos_c - x2 * sin_c, x2 * cos_c + x1 * sin_c,
             jnp.zeros((hpc, A_QK_PAD - A_QK, tm), F32)], axis=1)
        qt_ref[0, c * hpc:(c + 1) * hpc, 0] = full.astype(BF16)
        sl = slice(c * g_cols, (c + 1) * g_cols)
        gate_ref[:, sl] = _silu(_dot(h, win_ref[:, sl])).astype(BF16)

    ss_r = jnp.sum(kr * kr, axis=-1, keepdims=True)
    krt = (kr * gkr_ref[...]).T
    y1 = krt[:half]
    y2 = krt[half:A_ROPE]
    kr_rot = jnp.concatenate(
        [y1 * cos_t - y2 * sin_t, y2 * cos_t + y1 * sin_t, krt[A_ROPE:]], axis=0).T
    gkn = gkn_ref[...]
    kn_all = _dot(ckv, wk_ref[...])
    for hh in range(A_HEADS):
        kn = kn_all[:, hh * A_NOPE:(hh + 1) * A_NOPE]
        ss = jnp.sum(kn * kn, axis=-1, keepdims=True) + ss_r
        r = lax.rsqrt(ss * (1.0 / A_QK) + EPS)
        k_ref[0, hh, :, :A_NOPE] = (kn * gkn * r).astype(BF16)
        k_ref[0, hh, :, A_NOPE:] = (kr_rot * r).astype(BF16)

    for c in range(n_chunk):
        rows = slice(c * hpc * A_V, (c + 1) * hpc * A_V)
        vt = _dot_nt(wvt_ref[rows, :], ckv)
        vt_ref[0, c * hpc:(c + 1) * hpc, 0, :A_V, :] = vt.reshape(hpc, A_V, tm).astype(BF16)
    ones_row = lax.broadcasted_iota(jnp.int32, (A_HEADS, A_V_PAD - A_V, tm), 1) == 0
    vt_ref[0, :, 0, A_V:, :] = jnp.where(ones_row, 1.0, 0.0).astype(BF16)


def _mla_attn_kernel(qt_ref, k_ref, vt_ref, gate_ref, o_ref,
                     s0_ref, s1_ref, acc_ref, m_ref, *, tq, tk, tc):
    seq = k_ref.shape[2]
    tv = vt_ref.shape[4]
    assert tq == 2 * tk and qt_ref.shape[4] == tc and tk % tv == 0
    n_c = tq // tc

    def scores(s_ref, ks, qs, c):
        s_ref[c] = _dot(k_ref[0, 0, pl.ds(ks, tk), :], qt_ref[0, 0, qs // tc + c])

    def softmax_pv(s_ref, ks, c, mask):
        st = s_ref[c]
        if mask is not None:
            st = jnp.where(mask, st, NEG)
        m = m_ref[c]
        m_new = jnp.maximum(m, jnp.max(st, axis=0, keepdims=True))
        p = jnp.exp2(st - m_new).astype(BF16)
        alpha = jnp.exp2(m - m_new)
        m_ref[c] = m_new
        vt = jnp.concatenate([vt_ref[0, 0, ks // tv + i] for i in range(tk // tv)], axis=1)
        acc_ref[c] = alpha * acc_ref[c] + _dot(vt, p)

    def q_body(qi, _):
        qs = pl.multiple_of(qi * tq, tq)

        def pair(j, _):
            ks = pl.multiple_of(j * tq, tq)
            for c in range(n_c):
                scores(s1_ref, ks + tk, qs, c)
                softmax_pv(s0_ref, ks, c, None)
            for c in range(n_c):
                scores(s0_ref, ks + tq, qs, c)
                softmax_pv(s1_ref, ks + tk, c, None)
            return 0

        def finish(c):
            o = (acc_ref[c, :A_V] * (1.0 / acc_ref[c, A_V:A_V + 1])).T
            rows = pl.ds(qs + c * tc, tc)
            o_ref[rows, :] = (o * gate_ref[rows, :].astype(F32)).astype(BF16)

        acc_ref[...] = jnp.zeros_like(acc_ref)
        m_ref[...] = jnp.full_like(m_ref, NEG)
        one = qi & 1
        two = lax.shift_right_logical(qi, 1) & 1

        @pl.when(one == 1)
        def _():
            pair(0, 0)

        @pl.when(two == 1)
        def _():
            pair(one, 0)
            pair(one + 1, 0)

        def four_pairs(j4, _):
            base = one + 2 * two + 4 * j4
            for i in range(4):
                pair(base + i, 0)
            return 0

        lax.fori_loop(0, lax.shift_right_logical(qi, 2), four_pairs, 0)
        row = lax.broadcasted_iota(jnp.int32, (tk, tc), 0)
        col = lax.broadcasted_iota(jnp.int32, (tk, tc), 1)
        for c in range(n_c):
            if (c + 1) * tc > tk:
                scores(s1_ref, qs + tk, qs, c)
        qs_next = pl.multiple_of(jnp.minimum(qs + tq, seq - tq), tq)
        n_late = 2
        for c in range(n_c):
            softmax_pv(s0_ref, qs, c, None if c * tc >= tk else row <= col + c * tc)
            if (c + 1) * tc <= tk:
                finish(c)
            if c < n_c - n_late:
                scores(s0_ref, 0, qs_next, c)
        for c in range(n_c):
            if (c + 1) * tc > tk:
                softmax_pv(s1_ref, qs + tk, c, row + tk <= col + c * tc)
                finish(c)
        for c in range(n_c - n_late, n_c):
            scores(s0_ref, 0, qs_next, c)
        return 0

    for c in range(n_c):
        scores(s0_ref, 0, 0, c)
    lax.fori_loop(0, seq // tq, q_body, 0)


def _mla_out_kernel(og_ref, x_ref, wo_ref, gkv_ref, wkt_ref, wvt_ref, pos_ref, inv_ref, gk_ref,
                    x1_ref, k_ref, vt_ref):
    tm = x_ref.shape[0]
    x1 = x_ref[...] + _dot(og_ref[...], wo_ref[...].astype(BF16))
    x1_ref[...] = x1
    hk = _rms_rows(x1, gkv_ref[...]).astype(BF16)
    kt = _dot_nt(wkt_ref[...], hk).reshape(B_KV_HEADS, B_HD, tm)
    ss = jnp.sum(kt * kt, axis=1, keepdims=True)
    kn = kt * lax.rsqrt(ss * (1.0 / B_HD) + EPS) * gk_ref[...][None]
    ang = inv_ref[...] * pos_ref[0]
    c = jnp.cos(ang)[None]
    s = jnp.sin(ang)[None]
    x1r = kn[:, :B_ROPE_HALF]
    x2r = kn[:, B_ROPE_HALF:2 * B_ROPE_HALF]
    kfull = jnp.concatenate(
        [x1r * c - x2r * s, x2r * c + x1r * s, kn[:, 2 * B_ROPE_HALF:],
         jnp.zeros((B_KV_HEADS, B_K_PAD - B_HD, tm), F32)], axis=1)
    k_ref[0] = kfull.reshape(B_KV_HEADS * B_K_PAD, tm).T.astype(BF16)
    vt = _dot_nt(wvt_ref[...], hk).reshape(B_KV_HEADS, B_HD, tm)
    vt_ref[0, :, :B_HD, :] = vt.astype(BF16)
    ones_row = lax.broadcasted_iota(jnp.int32, (B_KV_HEADS, B_V_PAD - B_HD, tm), 1) == 0
    vt_ref[0, :, B_HD:, :] = jnp.where(ones_row, 1.0, 0.0).astype(BF16)


def _swa_attn_kernel(sink_ref, x1_ref, g_ref, win_ref, pos_ref, inv_ref, gq_ref,
                     kc_ref, kp_ref, vc_ref, vp_ref, o_ref,
                     hb_ref, qt_ref, gate_ref, ot_ref, *, q_scale):
    w = B_WINDOW
    tm = x1_ref.shape[0]
    n_blk = tm // w
    j = pl.program_id(1)
    gw = B_GROUP * w
    hb_ref[...] = _rms_rows(x1_ref[...], g_ref[...]).astype(BF16)

    ang = inv_ref[...] * pos_ref[0]
    cos_c = jnp.cos(ang)[None]
    sin_c = jnp.sin(ang)[None]
    gq = gq_ref[...][None]
    hpp = 8
    for p in range(B_HEADS // hpp):
        rows = slice(p * hpp * B_HD, (p + 1) * hpp * B_HD)
        qt = _dot(hb_ref[...], win_ref[:, rows]).T.reshape(hpp, B_HD, tm)
        ss = jnp.sum(qt * qt, axis=1, keepdims=True)
        qn = qt * (lax.rsqrt(ss * (1.0 / B_HD) + EPS) * q_scale) * gq
        x1r = qn[:, :B_ROPE_HALF]
        x2r = qn[:, B_ROPE_HALF:2 * B_ROPE_HALF]
        qfull = jnp.concatenate(
            [x1r * cos_c - x2r * sin_c, x2r * cos_c + x1r * sin_c, qn[:, 2 * B_ROPE_HALF:]], axis=1)
        qb = qfull.reshape(hpp * B_HD, tm).astype(BF16)
        for blk in range(n_blk):
            qt_ref[blk, rows, :] = qb[:, blk * w:(blk + 1) * w]

    g_rows = 2 * w
    g_cols = 256
    q_cols = B_HEADS * B_HD
    gate_pieces = [(r, c) for r in range(tm // g_rows) for c in range(gate_ref.shape[1] // g_cols)]

    def gate_piece(r, c):
        rs = slice(r * g_rows, (r + 1) * g_rows)
        cs = slice(c * g_cols, (c + 1) * g_cols)
        gate_ref[rs, cs] = _silu(
            _dot(hb_ref[rs, :], win_ref[:, q_cols + c * g_cols:q_cols + (c + 1) * g_cols])).astype(BF16)

    row = lax.broadcasted_iota(jnp.int32, (w, gw), 0)
    col = lax.broadcasted_iota(jnp.int32, (w, gw), 1)
    cur_visible = row <= (col & (w - 1))
    cur_visible_f = cur_visible.astype(F32)
    head_in_group = lax.broadcasted_iota(jnp.int32, (1, gw), 1) // w
    has_prev = j > 0

    def band_scores(blk, g):
        qs = slice(blk * w, (blk + 1) * w)
        ksl = slice(g * B_K_PAD, g * B_K_PAD + B_HD)
        if blk == 0:
            k_prev = kp_ref[0, :, ksl]
        else:
            k_prev = kc_ref[0, (blk - 1) * w:blk * w, ksl]
        k_band = jnp.concatenate([k_prev, kc_ref[0, qs, ksl]], axis=0)
        qt_g = jnp.concatenate(
            [qt_ref[blk, (B_GROUP * g + i) * B_HD:(B_GROUP * g + i + 1) * B_HD, :]
             for i in range(B_GROUP)], axis=1)
        return _dot(k_band, qt_g)

    def softmax_pv(st, blk, g):
        qs = slice(blk * w, (blk + 1) * w)
        s_prev = st[:w]
        if blk == 0:
            s_prev = jnp.where(has_prev, s_prev, NEG)
        s = jnp.where(cur_visible, st[w:], s_prev)
        sink = jnp.full((1, gw), sink_ref[B_GROUP * g] * LOG2E, F32)
        for i in range(1, B_GROUP):
            sink = jnp.where(head_in_group == i, sink_ref[B_GROUP * g + i] * LOG2E, sink)
        m = jnp.maximum(jnp.max(s, axis=0, keepdims=True), sink)
        e = jnp.exp2(s - m)
        e_cur = e * cur_visible_f
        p = jnp.concatenate([e - e_cur, e_cur], axis=0).astype(BF16)
        if blk == 0:
            v_prev = vp_ref[0, g]
        else:
            v_prev = vc_ref[0, g, :, (blk - 1) * w:blk * w]
        v_band = jnp.concatenate([v_prev, vc_ref[0, g, :, qs]], axis=1)
        pv = _dot(v_band, p)
        den = pv[B_HD:B_HD + 1] + jnp.exp2(sink - m)
        ot = pv[:B_HD] * (1.0 / den)
        for i in range(B_GROUP):
            hh = B_GROUP * g + i
            ot_ref[blk, hh * B_HD:(hh + 1) * B_HD, :] = ot[:, i * w:(i + 1) * w]

    def finish(blk):
        qs = slice(blk * w, (blk + 1) * w)
        o_ref[qs, :] = (ot_ref[blk].T * gate_ref[qs, :].astype(F32)).astype(BF16)

    order = [(blk, g) for blk in range(n_blk) for g in range(B_KV_HEADS)]
    assert len(order) == 2 * len(gate_pieces) and g_rows == 2 * w
    st = band_scores(*order[0])
    for idx, (blk, g) in enumerate(order):
        st_next = band_scores(*order[idx + 1]) if idx + 1 < len(order) else None
        softmax_pv(st, blk, g)
        if idx % 2 == 1:
            gate_piece(*gate_pieces[idx // 2])
        if g == B_KV_HEADS - 1 and blk % 2 == 1:
            finish(blk - 1)
            finish(blk)
        st = st_next


def _swa_out_kernel(og_ref, x1_ref, wo_ref, o_ref):
    o_ref[...] = x1_ref[...] + _dot(og_ref[...], wo_ref[...].astype(BF16))


def kernel(x, positions, a_norm, a_w_in, a_q_latent_norm, a_w_q_up, a_kv_latent_norm, a_w_kv_up,
           a_q_head_norm, a_k_head_norm, a_w_o, kv_norm, kv_w, kv_k_head_norm, b_norm, b_w_in,
           b_q_head_norm, b_sinks, b_w_o):
    bsz, seq, d = x.shape
    t = bsz * seq
    assert a_norm.shape[0] == 1 and b_norm.shape[0] == 1
    assert seq % TQ_A == 0 and seq % QB_SWA == 0 and TQ_A % TK_A == 0

    xf = x.reshape(t, d)
    pos_row = positions.reshape(bsz, 1, seq).astype(F32)

    w_in = a_w_in[0]
    n_lat = 2 * A_LORA + A_ROPE
    win = jnp.concatenate(
        [w_in[:, n_lat:], w_in[:, :n_lat], jnp.zeros((d, 128 - A_ROPE), F32)], axis=1).astype(BF16)
    wqt = a_w_q_up[0].astype(BF16).T
    wkv = a_w_kv_up[0].astype(BF16).reshape(A_LORA, A_HEADS, A_NOPE + A_V)
    wk = wkv[:, :, :A_NOPE].reshape(A_LORA, A_HEADS * A_NOPE)
    wvt = wkv[:, :, A_NOPE:].reshape(A_LORA, A_HEADS * A_V).T
    gq_col = jnp.broadcast_to(a_q_head_norm[0][:, None], (A_QK, TM_UP))
    gkn = a_k_head_norm[0][:A_NOPE].reshape(1, A_NOPE)
    gkr = jnp.pad(a_k_head_norm[0][A_NOPE:], (0, 128 - A_ROPE)).reshape(1, 128)
    inv_a = jnp.power(jnp.float32(ROPE_THETA), -jnp.arange(0, A_ROPE, 2, dtype=F32) / A_ROPE)
    inv_a_col = jnp.broadcast_to(inv_a[:, None], (A_ROPE // 2, TM_UP))
    wo_a = a_w_o[0]

    kv_dim = B_KV_HEADS * B_HD
    kv_w16 = kv_w.astype(BF16)
    wkt_b = kv_w16[:, :kv_dim].T
    wvt_b = kv_w16[:, kv_dim:].T
    rot_b = 2 * B_ROPE_HALF
    inv_b = jnp.power(jnp.float32(ROPE_THETA), -jnp.arange(0, rot_b, 2, dtype=F32) / rot_b)
    q_dim = B_HEADS * B_HD
    win_b = b_w_in[0].astype(BF16)
    wo_b = b_w_o[0]

    ns_up = seq // TM_UP
    tok_up = lambda b, j: (b * ns_up + j, 0)
    qt_a, k_a, vt_a, gate_a = pl.pallas_call(
        functools.partial(_mla_proj_kernel, q_scale=A_QK ** -0.5 * LOG2E),
        grid=(bsz, ns_up),
        in_specs=[
            pl.BlockSpec((TM_UP, d), tok_up),
            _const_spec((1, d)),
            _const_spec(win.shape),
            _const_spec((1, A_LORA)),
            _const_spec((1, A_LORA)),
            pl.BlockSpec((1, 1, TM_UP), lambda b, j: (b, 0, j)),
            _const_spec(inv_a_col.shape),
            _const_spec(wqt.shape),
            _const_spec(wk.shape),
            _const_spec(wvt.shape),
            _const_spec(gq_col.shape),
            _const_spec((1, A_NOPE)),
            _const_spec((1, 128)),
        ],
        out_specs=[
            pl.BlockSpec((1, A_HEADS, 1, A_QK_PAD, TM_UP), lambda b, j: (b, 0, j, 0, 0)),
            pl.BlockSpec((1, A_HEADS, TM_UP, A_QK_PAD), lambda b, j: (b, 0, j, 0)),
            pl.BlockSpec((1, A_HEADS, 1, A_V_PAD, TM_UP), lambda b, j: (b, 0, j, 0, 0)),
            pl.BlockSpec((TM_UP, d), tok_up),
        ],
        out_shape=[
            jax.ShapeDtypeStruct((bsz, A_HEADS, ns_up, A_QK_PAD, TM_UP), BF16),
            jax.ShapeDtypeStruct((bsz, A_HEADS, seq, A_QK_PAD), BF16),
            jax.ShapeDtypeStruct((bsz, A_HEADS, ns_up, A_V_PAD, TM_UP), BF16),
            jax.ShapeDtypeStruct((t, A_HEADS * A_V), BF16),
        ],
        compiler_params=_params(2),
        name="mla_proj",
    )(xf, a_norm[0].reshape(1, d), win, a_q_latent_norm[0].reshape(1, A_LORA),
      a_kv_latent_norm[0].reshape(1, A_LORA), pos_row, inv_a_col, wqt, wk, wvt, gq_col, gkn, gkr)

    og_a = pl.pallas_call(
        functools.partial(_mla_attn_kernel, tq=TQ_A, tk=TK_A, tc=TC_A),
        grid=(bsz, A_HEADS),
        in_specs=[
            pl.BlockSpec((1, 1, ns_up, A_QK_PAD, TM_UP), lambda b, h: (b, h, 0, 0, 0)),
            pl.BlockSpec((1, 1, seq, A_QK_PAD), lambda b, h: (b, h, 0, 0)),
            pl.BlockSpec((1, 1, ns_up, A_V_PAD, TM_UP), lambda b, h: (b, h, 0, 0, 0)),
            pl.BlockSpec((seq, A_V), lambda b, h: (b, h)),
        ],
        out_specs=pl.BlockSpec((seq, A_V), lambda b, h: (b, h)),
        out_shape=jax.ShapeDtypeStruct((t, A_HEADS * A_V), BF16),
        scratch_shapes=[pltpu.VMEM((TQ_A // TC_A, TK_A, TC_A), F32), pltpu.VMEM((TQ_A // TC_A, TK_A, TC_A), F32),
                        pltpu.VMEM((TQ_A // TC_A, A_V_PAD, TC_A), F32), pltpu.VMEM((TQ_A // TC_A, 1, TC_A), F32)],
        compiler_params=_params(2),
        name="mla_attn",
    )(qt_a, k_a, vt_a, gate_a)

    ns_out = seq // TM_OUT
    tok_out = lambda b, j: (b * ns_out + j, 0)
    x1, k_b, vt_b = pl.pallas_call(
        _mla_out_kernel,
        grid=(bsz, ns_out),
        in_specs=[
            pl.BlockSpec((TM_OUT, d), tok_out),
            pl.BlockSpec((TM_OUT, d), tok_out),
            _const_spec(wo_a.shape),
            _const_spec((1, d)),
            _const_spec(wkt_b.shape),
            _const_spec(wvt_b.shape),
            pl.BlockSpec((1, 1, TM_OUT), lambda b, j: (b, 0, j)),
            _const_spec((B_ROPE_HALF, TM_OUT)),
            _const_spec((B_HD, TM_OUT)),
        ],
        out_specs=[
            pl.BlockSpec((TM_OUT, d), tok_out),
            pl.BlockSpec((1, TM_OUT, B_KV_HEADS * B_K_PAD), lambda b, j: (b, j, 0)),
            pl.BlockSpec((1, B_KV_HEADS, B_V_PAD, TM_OUT), lambda b, j: (b, 0, 0, j)),
        ],
        out_shape=[
            jax.ShapeDtypeStruct((t, d), F32),
            jax.ShapeDtypeStruct((bsz, seq, B_KV_HEADS * B_K_PAD), BF16),
            jax.ShapeDtypeStruct((bsz, B_KV_HEADS, B_V_PAD, seq), BF16),
        ],
        compiler_params=_params(2),
        name="mla_out",
    )(og_a, xf, wo_a, kv_norm.reshape(1, d), wkt_b, wvt_b, pos_row,
      jnp.broadcast_to(inv_b[:, None], (B_ROPE_HALF, TM_OUT)),
      jnp.broadcast_to(kv_k_head_norm[:, None], (B_HD, TM_OUT)))

    ns_sw = seq // QB_SWA
    blk_per_step = QB_SWA // B_WINDOW
    prev_blk = lambda b, j: jnp.maximum(j * blk_per_step - 1, 0)
    og_b = pl.pallas_call(
        functools.partial(_swa_attn_kernel, q_scale=B_HD ** -0.5 * LOG2E),
        grid=(bsz, ns_sw),
        in_specs=[
            pl.BlockSpec(memory_space=pltpu.SMEM),
            pl.BlockSpec((QB_SWA, d), lambda b, j: (b * ns_sw + j, 0)),
            _const_spec((1, d)),
            _const_spec(win_b.shape),
            pl.BlockSpec((1, 1, QB_SWA), lambda b, j: (b, 0, j)),
            _const_spec((B_ROPE_HALF, QB_SWA)),
            _const_spec((B_HD, QB_SWA)),
            pl.BlockSpec((1, QB_SWA, B_KV_HEADS * B_K_PAD), lambda b, j: (b, j, 0)),
            pl.BlockSpec((1, B_WINDOW, B_KV_HEADS * B_K_PAD), lambda b, j: (b, prev_blk(b, j), 0)),
            pl.BlockSpec((1, B_KV_HEADS, B_V_PAD, QB_SWA), lambda b, j: (b, 0, 0, j)),
            pl.BlockSpec((1, B_KV_HEADS, B_V_PAD, B_WINDOW), lambda b, j: (b, 0, 0, prev_blk(b, j))),
        ],
        out_specs=pl.BlockSpec((QB_SWA, q_dim), lambda b, j: (b * ns_sw + j, 0)),
        out_shape=jax.ShapeDtypeStruct((t, q_dim), BF16),
        scratch_shapes=[pltpu.VMEM((QB_SWA, d), BF16),
                        pltpu.VMEM((blk_per_step, q_dim, B_WINDOW), BF16),
                        pltpu.VMEM((QB_SWA, q_dim), BF16),
                        pltpu.VMEM((blk_per_step, q_dim, B_WINDOW), F32)],
        compiler_params=_params(2),
        name="swa_attn",
    )(b_sinks[0], x1, b_norm[0].reshape(1, d), win_b, pos_row,
      jnp.broadcast_to(inv_b[:, None], (B_ROPE_HALF, QB_SWA)),
      jnp.broadcast_to(b_q_head_norm[0][:, None], (B_HD, QB_SWA)),
      k_b, k_b, vt_b, vt_b)

    n_out = t // TM_OUT
    out = pl.pallas_call(
        _swa_out_kernel,
        grid=(n_out,),
        in_specs=[
            pl.BlockSpec((TM_OUT, q_dim), lambda i: (i, 0)),
            pl.BlockSpec((TM_OUT, d), lambda i: (i, 0)),
            _const_spec(wo_b.shape),
        ],
        out_specs=pl.BlockSpec((TM_OUT, d), lambda i: (i, 0)),
        out_shape=jax.ShapeDtypeStruct((t, d), F32),
        compiler_params=_params(1),
        name="swa_out",
    )(og_b, x1, wo_b)
    return out.reshape(bsz, seq, d)
```

```python
import functools
import math

import jax
import jax.numpy as jnp
from jax import lax
from jax.experimental import pallas as pl
from jax.experimental.pallas import tpu as pltpu

F32 = jnp.float32
BF16 = jnp.bfloat16

EPS = 1e-6
ROPE_THETA = 500000.0
NEG = -1e30
LOG2E = math.log2(math.e)

A_HEADS = 16
A_LORA = 512
A_NOPE = 128
A_ROPE = 64
A_QK = A_NOPE + A_ROPE
A_QK_PAD = 256
A_V = 128
A_V_PAD = A_V + 16

B_HEADS = 32
B_KV_HEADS = 8
B_GROUP = B_HEADS // B_KV_HEADS
B_HD = 64
B_ROPE_HALF = 8
B_WINDOW = 128
B_K_PAD = 128
B_V_PAD = B_HD + 16

VMEM_LIMIT = 56 * 1024 * 1024

TM_UP = 256
TQ_A = 1024
TK_A = 512
TC_A = 256
TM_OUT = 512
QB_SWA = 512


def _const_spec(shape):
    return pl.BlockSpec(shape, lambda *_: (0,) * len(shape), pipeline_mode=pl.Buffered(1))


def _params(n_axes):
    return pltpu.CompilerParams(
        dimension_semantics=("arbitrary",) * n_axes, vmem_limit_bytes=VMEM_LIMIT)


def _rms_rows(xf, g_row):
    ms = jnp.mean(xf * xf, axis=-1, keepdims=True)
    return xf * lax.rsqrt(ms + EPS) * g_row


def _dot(a, b):
    return jnp.dot(a, b, preferred_element_type=F32)


def _dot_nt(a, b):
    return lax.dot_general(a, b, (((1,), (1,)), ((), ())), preferred_element_type=F32)


def _silu(g):
    return g * jax.nn.sigmoid(g)


def _mla_proj_kernel(x_ref, g_ref, wl_ref, wg_ref, glq_ref, glkv_ref, posr_ref, invc_ref,
                     wqt_ref, wk_ref, wvt_ref, gqc_ref, gkn_ref, gkr_ref,
                     qt_ref, k_ref, vt_ref, gate_ref, *, q_scale):
    tm = x_ref.shape[0]
    h = _rms_rows(x_ref[...], g_ref[...]).astype(BF16)
    lat = _dot(h, wl_ref[...])
    cq = _rms_rows(lat[:, :A_LORA], glq_ref[...]).astype(BF16)
    ckv = _rms_rows(lat[:, A_LORA:2 * A_LORA], glkv_ref[...]).astype(BF16)
    kr = lat[:, 2 * A_LORA:]

    ang_t = invc_ref[...] * posr_ref[0]
    cos_t = jnp.cos(ang_t)
    sin_t = jnp.sin(ang_t)
    cos_c = cos_t[None]
    sin_c = sin_t[None]
    gqc = gqc_ref[...][None]
    hpc = 4
    n_chunk = A_HEADS // hpc
    g_cols = gate_ref.shape[1] // n_chunk
    half = A_ROPE // 2
    for c in range(n_chunk):
        rows = slice(c * hpc * A_QK, (c + 1) * hpc * A_QK)
        qt = _dot_nt(wqt_ref[rows, :], cq).reshape(hpc, A_QK, tm)
        ss = jnp.sum(qt * qt, axis=1, keepdims=True)
        qn = qt * (lax.rsqrt(ss * (1.0 / A_QK) + EPS) * q_scale) * gqc
        x1 = qn[:, A_NOPE:A_NOPE + half]
        x2 = qn[:, A_NOPE + half:]
        full = jnp.concatenate(
            [qn[:, :A_NOPE], x1 * cos_c - x2 * sin_c, x2 * cos_c + x1 * sin_c,
             jnp.zeros((hpc, A_QK_PAD - A_QK, tm), F32)], axis=1)
        qt_ref[0, c * hpc:(c + 1) * hpc, 0] = full.astype(BF16)
        sl = slice(c * g_cols, (c + 1) * g_cols)
        gate_ref[:, sl] = _silu(_dot(h, wg_ref[:, sl])).astype(BF16)

    ss_r = jnp.sum(kr * kr, axis=-1, keepdims=True)
    krt = (kr * gkr_ref[...]).T
    y1 = krt[:half]
    y2 = krt[half:A_ROPE]
    kr_rot = jnp.concatenate(
        [y1 * cos_t - y2 * sin_t, y2 * cos_t + y1 * sin_t, krt[A_ROPE:]], axis=0).T
    gkn = gkn_ref[...]
    kn_all = _dot(ckv, wk_ref[...])
    for hh in range(A_HEADS):
        kn = kn_all[:, hh * A_NOPE:(hh + 1) * A_NOPE]
        ss = jnp.sum(kn * kn, axis=-1, keepdims=True) + ss_r
        r = lax.rsqrt(ss * (1.0 / A_QK) + EPS)
        k_ref[0, hh, :, :A_NOPE] = (kn * gkn * r).astype(BF16)
        k_ref[0, hh, :, A_NOPE:] = (kr_rot * r).astype(BF16)

    for c in range(n_chunk):
        rows = slice(c * hpc * A_V, (c + 1) * hpc * A_V)
        vt = _dot_nt(wvt_ref[rows, :], ckv)
        vt_ref[0, c * hpc:(c + 1) * hpc, 0, :A_V, :] = vt.reshape(hpc, A_V, tm).astype(BF16)
    ones_row = lax.broadcasted_iota(jnp.int32, (A_HEADS, A_V_PAD - A_V, tm), 1) == 0
    vt_ref[0, :, 0, A_V:, :] = jnp.where(ones_row, 1.0, 0.0).astype(BF16)


def _mla_attn_kernel(qt_ref, k_ref, vt_ref, gate_ref, o_ref,
                     s0_ref, s1_ref, acc_ref, m_ref, *, tq, tk, tc):
    seq = k_ref.shape[2]
    tv = vt_ref.shape[4]
    assert tq == 2 * tk and qt_ref.shape[4] == tc and tk % tv == 0
    n_c = tq // tc

    def scores(s_ref, ks, qs, c):
        s_ref[c] = _dot(k_ref[0, 0, pl.ds(ks, tk), :], qt_ref[0, 0, qs // tc + c])

    def softmax_pv(s_ref, ks, c, mask):
        st = s_ref[c]
        if mask is not None:
            st = jnp.where(mask, st, NEG)
        m = m_ref[c]
        m_new = jnp.maximum(m, jnp.max(st, axis=0, keepdims=True))
        p = jnp.exp2(st - m_new).astype(BF16)
        alpha = jnp.exp2(m - m_new)
        m_ref[c] = m_new
        vt = jnp.concatenate([vt_ref[0, 0, ks // tv + i] for i in range(tk // tv)], axis=1)
        acc_ref[c] = alpha * acc_ref[c] + _dot(vt, p)

    def q_body(qi, _):
        qs = pl.multiple_of(qi * tq, tq)

        def pair(j, _):
            ks = pl.multiple_of(j * tq, tq)
            for c in range(n_c):
                scores(s1_ref, ks + tk, qs, c)
                softmax_pv(s0_ref, ks, c, None)
            for c in range(n_c):
                scores(s0_ref, ks + tq, qs, c)
                softmax_pv(s1_ref, ks + tk, c, None)
            return 0

        def finish(c):
            o = (acc_ref[c, :A_V] * (1.0 / acc_ref[c, A_V:A_V + 1])).T
            rows = pl.ds(qs + c * tc, tc)
            o_ref[rows, :] = (o * gate_ref[rows, :].astype(F32)).astype(BF16)

        acc_ref[...] = jnp.zeros_like(acc_ref)
        m_ref[...] = jnp.full_like(m_ref, NEG)
        one = qi & 1
        two = lax.shift_right_logical(qi, 1) & 1

        @pl.when(two == 1)
        def _():
            pair(0, 0)
            pair(1, 0)

        def four_pairs(j4, _):
            base = 2 * two + 4 * j4
            for i in range(4):
                pair(base + i, 0)
            return 0

        lax.fori_loop(0, lax.shift_right_logical(qi, 2), four_pairs, 0)

        def diagonal():
            row = lax.broadcasted_iota(jnp.int32, (tk, tc), 0)
            col = lax.broadcasted_iota(jnp.int32, (tk, tc), 1)
            for c in range(n_c):
                if (c + 1) * tc > tk:
                    scores(s1_ref, qs + tk, qs, c)
            qs_next = pl.multiple_of(jnp.minimum(qs + tq, seq - tq), tq)
            n_late = 2
            for c in range(n_c):
                softmax_pv(s0_ref, qs, c, None if c * tc >= tk else row <= col + c * tc)
                if (c + 1) * tc <= tk:
                    finish(c)
                if c < n_c - n_late:
                    scores(s0_ref, 0, qs_next, c)
            for c in range(n_c):
                if (c + 1) * tc > tk:
                    softmax_pv(s1_ref, qs + tk, c, row + tk <= col + c * tc)
                    finish(c)
            for c in range(n_c - n_late, n_c):
                scores(s0_ref, 0, qs_next, c)

        @pl.when(one == 1)
        def _():
            pair(qi - 1, 0)
            diagonal()

        @pl.when(one == 0)
        def _():
            diagonal()

        return 0

    for c in range(n_c):
        scores(s0_ref, 0, 0, c)
    lax.fori_loop(0, seq // tq, q_body, 0)


def _mla_out_kernel(og_ref, x_ref, wo_ref, gkv_ref, wkt_ref, wvt_ref, pos_ref, inv_ref, gk_ref,
                    x1_ref, k_ref, vt_ref):
    tm = x_ref.shape[0]
    x1 = x_ref[...] + _dot(og_ref[...], wo_ref[...].astype(BF16))
    x1_ref[...] = x1
    hk = _rms_rows(x1, gkv_ref[...]).astype(BF16)
    kt = _dot_nt(wkt_ref[...], hk).reshape(B_KV_HEADS, B_HD, tm)
    ss = jnp.sum(kt * kt, axis=1, keepdims=True)
    kn = kt * lax.rsqrt(ss * (1.0 / B_HD) + EPS) * gk_ref[...][None]
    ang = inv_ref[...] * pos_ref[0]
    c = jnp.cos(ang)[None]
    s = jnp.sin(ang)[None]
    x1r = kn[:, :B_ROPE_HALF]
    x2r = kn[:, B_ROPE_HALF:2 * B_ROPE_HALF]
    kfull = jnp.concatenate(
        [x1r * c - x2r * s, x2r * c + x1r * s, kn[:, 2 * B_ROPE_HALF:],
         jnp.zeros((B_KV_HEADS, B_K_PAD - B_HD, tm), F32)], axis=1)
    k_ref[0] = kfull.reshape(B_KV_HEADS * B_K_PAD, tm).T.astype(BF16)
    vt = _dot_nt(wvt_ref[...], hk).reshape(B_KV_HEADS, B_HD, tm)
    vt_ref[0, :, :B_HD, :] = vt.astype(BF16)
    ones_row = lax.broadcasted_iota(jnp.int32, (B_KV_HEADS, B_V_PAD - B_HD, tm), 1) == 0
    vt_ref[0, :, B_HD:, :] = jnp.where(ones_row, 1.0, 0.0).astype(BF16)


def _swa_attn_kernel(sink_ref, x1_ref, g_ref, win_ref, pos_ref, inv_ref, gq_ref,
                     kc_ref, kp_ref, vc_ref, vp_ref, o_ref,
                     hb_ref, qt_ref, gate_ref, ot_ref, *, q_scale):
    w = B_WINDOW
    tm = x1_ref.shape[0]
    n_blk = tm // w
    j = pl.program_id(1)
    gw = B_GROUP * w
    hb_ref[...] = _rms_rows(x1_ref[...], g_ref[...]).astype(BF16)

    ang = inv_ref[...] * pos_ref[0]
    cos_c = jnp.cos(ang)[None]
    sin_c = jnp.sin(ang)[None]
    gq = gq_ref[...][None]
    hpp = 8
    for p in range(B_HEADS // hpp):
        rows = slice(p * hpp * B_HD, (p + 1) * hpp * B_HD)
        qt = _dot(hb_ref[...], win_ref[:, rows]).T.reshape(hpp, B_HD, tm)
        ss = jnp.sum(qt * qt, axis=1, keepdims=True)
        qn = qt * (lax.rsqrt(ss * (1.0 / B_HD) + EPS) * q_scale) * gq
        x1r = qn[:, :B_ROPE_HALF]
        x2r = qn[:, B_ROPE_HALF:2 * B_ROPE_HALF]
        qfull = jnp.concatenate(
            [x1r * cos_c - x2r * sin_c, x2r * cos_c + x1r * sin_c, qn[:, 2 * B_ROPE_HALF:]], axis=1)
        qb = qfull.reshape(hpp * B_HD, tm).astype(BF16)
        for blk in range(n_blk):
            qt_ref[blk, rows, :] = qb[:, blk * w:(blk + 1) * w]

    g_rows = 2 * w
    g_cols = 256
    q_cols = B_HEADS * B_HD
    gate_pieces = [(r, c) for r in range(tm // g_rows) for c in range(gate_ref.shape[1] // g_cols)]

    def gate_piece(r, c):
        rs = slice(r * g_rows, (r + 1) * g_rows)
        cs = slice(c * g_cols, (c + 1) * g_cols)
        gate_ref[rs, cs] = _silu(
            _dot(hb_ref[rs, :], win_ref[:, q_cols + c * g_cols:q_cols + (c + 1) * g_cols])).astype(BF16)

    row = lax.broadcasted_iota(jnp.int32, (w, gw), 0)
    col = lax.broadcasted_iota(jnp.int32, (w, gw), 1)
    cur_visible = row <= (col & (w - 1))
    cur_visible_f = cur_visible.astype(F32)
    head_in_group = lax.broadcasted_iota(jnp.int32, (1, gw), 1) // w
    has_prev = j > 0

    def band_scores(blk, g):
        qs = slice(blk * w, (blk + 1) * w)
        ksl = slice(g * B_K_PAD, g * B_K_PAD + B_HD)
        if blk == 0:
            k_prev = kp_ref[0, :, ksl]
        else:
            k_prev = kc_ref[0, (blk - 1) * w:blk * w, ksl]
        k_band = jnp.concatenate([k_prev, kc_ref[0, qs, ksl]], axis=0)
        qt_g = jnp.concatenate(
            [qt_ref[blk, (B_GROUP * g + i) * B_HD:(B_GROUP * g + i + 1) * B_HD, :]
             for i in range(B_GROUP)], axis=1)
        return _dot(k_band, qt_g)

    def softmax_pv(st, blk, g):
        qs = slice(blk * w, (blk + 1) * w)
        s_prev = st[:w]
        if blk == 0:
            s_prev = jnp.where(has_prev, s_prev, NEG)
        s = jnp.where(cur_visible, st[w:], s_prev)
        sink = jnp.full((1, gw), sink_ref[B_GROUP * g] * LOG2E, F32)
        for i in range(1, B_GROUP):
            sink = jnp.where(head_in_group == i, sink_ref[B_GROUP * g + i] * LOG2E, sink)
        m = jnp.maximum(jnp.max(s, axis=0, keepdims=True), sink)
        e = jnp.exp2(s - m)
        e_cur = e * cur_visible_f
        p = jnp.concatenate([e - e_cur, e_cur], axis=0).astype(BF16)
        if blk == 0:
            v_prev = vp_ref[0, g]
        else:
            v_prev = vc_ref[0, g, :, (blk - 1) * w:blk * w]
        v_band = jnp.concatenate([v_prev, vc_ref[0, g, :, qs]], axis=1)
        pv = _dot(v_band, p)
        den = pv[B_HD:B_HD + 1] + jnp.exp2(sink - m)
        ot = pv[:B_HD] * (1.0 / den)
        for i in range(B_GROUP):
            hh = B_GROUP * g + i
            ot_ref[blk, hh * B_HD:(hh + 1) * B_HD, :] = ot[:, i * w:(i + 1) * w]

    def finish(blk):
        qs = slice(blk * w, (blk + 1) * w)
        o_ref[qs, :] = (ot_ref[blk].T * gate_ref[qs, :].astype(F32)).astype(BF16)

    order = [(blk, g) for blk in range(n_blk) for g in range(B_KV_HEADS)]
    assert len(order) == 2 * len(gate_pieces) and g_rows == 2 * w
    st = band_scores(*order[0])
    for idx, (blk, g) in enumerate(order):
        st_next = band_scores(*order[idx + 1]) if idx + 1 < len(order) else None
        softmax_pv(st, blk, g)
        if idx % 2 == 1:
            gate_piece(*gate_pieces[idx // 2])
        if g == B_KV_HEADS - 1 and blk % 2 == 1:
            finish(blk - 1)
            finish(blk)
        st = st_next


def _swa_out_kernel(og_ref, x1_ref, wo_ref, o_ref):
    o_ref[...] = x1_ref[...] + _dot(og_ref[...], wo_ref[...].astype(BF16))


def kernel(x, positions, a_norm, a_w_in, a_q_latent_norm, a_w_q_up, a_kv_latent_norm, a_w_kv_up,
           a_q_head_norm, a_k_head_norm, a_w_o, kv_norm, kv_w, kv_k_head_norm, b_norm, b_w_in,
           b_q_head_norm, b_sinks, b_w_o):
    bsz, seq, d = x.shape
    t = bsz * seq
    assert a_norm.shape[0] == 1 and b_norm.shape[0] == 1
    assert seq % TQ_A == 0 and seq % QB_SWA == 0 and TQ_A % TK_A == 0

    xf = x.reshape(t, d)
    pos_row = positions.reshape(bsz, 1, seq).astype(F32)

    w_in = a_w_in[0]
    n_lat = 2 * A_LORA + A_ROPE
    wl = jnp.pad(w_in[:, :n_lat], ((0, 0), (0, 128 - A_ROPE))).astype(BF16)
    wg = w_in[:, n_lat:].astype(BF16)
    wqt = a_w_q_up[0].astype(BF16).T
    wkv = a_w_kv_up[0].astype(BF16).reshape(A_LORA, A_HEADS, A_NOPE + A_V)
    wk = wkv[:, :, :A_NOPE].reshape(A_LORA, A_HEADS * A_NOPE)
    wvt = wkv[:, :, A_NOPE:].reshape(A_LORA, A_HEADS * A_V).T
    gq_col = jnp.broadcast_to(a_q_head_norm[0][:, None], (A_QK, TM_UP))
    gkn = a_k_head_norm[0][:A_NOPE].reshape(1, A_NOPE)
    gkr = jnp.pad(a_k_head_norm[0][A_NOPE:], (0, 128 - A_ROPE)).reshape(1, 128)
    inv_a = jnp.power(jnp.float32(ROPE_THETA), -jnp.arange(0, A_ROPE, 2, dtype=F32) / A_ROPE)
    inv_a_col = jnp.broadcast_to(inv_a[:, None], (A_ROPE // 2, TM_UP))
    wo_a = a_w_o[0]

    kv_dim = B_KV_HEADS * B_HD
    kv_w16 = kv_w.astype(BF16)
    wkt_b = kv_w16[:, :kv_dim].T
    wvt_b = kv_w16[:, kv_dim:].T
    rot_b = 2 * B_ROPE_HALF
    inv_b = jnp.power(jnp.float32(ROPE_THETA), -jnp.arange(0, rot_b, 2, dtype=F32) / rot_b)
    q_dim = B_HEADS * B_HD
    win_b = b_w_in[0].astype(BF16)
    wo_b = b_w_o[0]

    ns_up = seq // TM_UP
    tok_up = lambda b, j: (b * ns_up + j, 0)
    qt_a, k_a, vt_a, gate_a = pl.pallas_call(
        functools.partial(_mla_proj_kernel, q_scale=A_QK ** -0.5 * LOG2E),
        grid=(bsz, ns_up),
        in_specs=[
            pl.BlockSpec((TM_UP, d), tok_up),
            _const_spec((1, d)),
            _const_spec(wl.shape),
            _const_spec(wg.shape),
            _const_spec((1, A_LORA)),
            _const_spec((1, A_LORA)),
            pl.BlockSpec((1, 1, TM_UP), lambda b, j: (b, 0, j)),
            _const_spec(inv_a_col.shape),
            _const_spec(wqt.shape),
            _const_spec(wk.shape),
            _const_spec(wvt.shape),
            _const_spec(gq_col.shape),
            _const_spec((1, A_NOPE)),
            _const_spec((1, 128)),
        ],
        out_specs=[
            pl.BlockSpec((1, A_HEADS, 1, A_QK_PAD, TM_UP), lambda b, j: (b, 0, j, 0, 0)),
            pl.BlockSpec((1, A_HEADS, TM_UP, A_QK_PAD), lambda b, j: (b, 0, j, 0)),
            pl.BlockSpec((1, A_HEADS, 1, A_V_PAD, TM_UP), lambda b, j: (b, 0, j, 0, 0)),
            pl.BlockSpec((TM_UP, d), tok_up),
        ],
        out_shape=[
            jax.ShapeDtypeStruct((bsz, A_HEADS, ns_up, A_QK_PAD, TM_UP), BF16),
            jax.ShapeDtypeStruct((bsz, A_HEADS, seq, A_QK_PAD), BF16),
            jax.ShapeDtypeStruct((bsz, A_HEADS, ns_up, A_V_PAD, TM_UP), BF16),
            jax.ShapeDtypeStruct((t, A_HEADS * A_V), BF16),
        ],
        compiler_params=_params(2),
        name="mla_proj",
    )(xf, a_norm[0].reshape(1, d), wl, wg, a_q_latent_norm[0].reshape(1, A_LORA),
      a_kv_latent_norm[0].reshape(1, A_LORA), pos_row, inv_a_col, wqt, wk, wvt, gq_col, gkn, gkr)

    og_a = pl.pallas_call(
        functools.partial(_mla_attn_kernel, tq=TQ_A, tk=TK_A, tc=TC_A),
        grid=(bsz, A_HEADS),
        in_specs=[
            pl.BlockSpec((1, 1, ns_up, A_QK_PAD, TM_UP), lambda b, h: (b, h, 0, 0, 0)),
            pl.BlockSpec((1, 1, seq, A_QK_PAD), lambda b, h: (b, h, 0, 0)),
            pl.BlockSpec((1, 1, ns_up, A_V_PAD, TM_UP), lambda b, h: (b, h, 0, 0, 0)),
            pl.BlockSpec((seq, A_V), lambda b, h: (b, h)),
        ],
        out_specs=pl.BlockSpec((seq, A_V), lambda b, h: (b, h)),
        out_shape=jax.ShapeDtypeStruct((t, A_HEADS * A_V), BF16),
        scratch_shapes=[pltpu.VMEM((TQ_A // TC_A, TK_A, TC_A), F32), pltpu.VMEM((TQ_A // TC_A, TK_A, TC_A), F32),
                        pltpu.VMEM((TQ_A // TC_A, A_V_PAD, TC_A), F32), pltpu.VMEM((TQ_A // TC_A, 1, TC_A), F32)],
        compiler_params=_params(2),
        name="mla_attn",
    )(qt_a, k_a, vt_a, gate_a)

    ns_out = seq // TM_OUT
    tok_out = lambda b, j: (b * ns_out + j, 0)
    x1, k_b, vt_b = pl.pallas_call(
        _mla_out_kernel,
        grid=(bsz, ns_out),
        in_specs=[
            pl.BlockSpec((TM_OUT, d), tok_out),
            pl.BlockSpec((TM_OUT, d), tok_out),
            _const_spec(wo_a.shape),
            _const_spec((1, d)),
            _const_spec(wkt_b.shape),
            _const_spec(wvt_b.shape),
            pl.BlockSpec((1, 1, TM_OUT), lambda b, j: (b, 0, j)),
            _const_spec((B_ROPE_HALF, TM_OUT)),
            _const_spec((B_HD, TM_OUT)),
        ],
        out_specs=[
            pl.BlockSpec((TM_OUT, d), tok_out),
            pl.BlockSpec((1, TM_OUT, B_KV_HEADS * B_K_PAD), lambda b, j: (b, j, 0)),
            pl.BlockSpec((1, B_KV_HEADS, B_V_PAD, TM_OUT), lambda b, j: (b, 0, 0, j)),
        ],
        out_shape=[
            jax.ShapeDtypeStruct((t, d), F32),
            jax.ShapeDtypeStruct((bsz, seq, B_KV_HEADS * B_K_PAD), BF16),
            jax.ShapeDtypeStruct((bsz, B_KV_HEADS, B_V_PAD, seq), BF16),
        ],
        compiler_params=_params(2),
        name="mla_out",
    )(og_a, xf, wo_a, kv_norm.reshape(1, d), wkt_b, wvt_b, pos_row,
      jnp.broadcast_to(inv_b[:, None], (B_ROPE_HALF, TM_OUT)),
      jnp.broadcast_to(kv_k_head_norm[:, None], (B_HD, TM_OUT)))

    ns_sw = seq // QB_SWA
    blk_per_step = QB_SWA // B_WINDOW
    prev_blk = lambda b, j: jnp.maximum(j * blk_per_step - 1, 0)
    og_b = pl.pallas_call(
        functools.partial(_swa_attn_kernel, q_scale=B_HD ** -0.5 * LOG2E),
        grid=(bsz, ns_sw),
        in_specs=[
            pl.BlockSpec(memory_space=pltpu.SMEM),
            pl.BlockSpec((QB_SWA, d), lambda b, j: (b * ns_sw + j, 0)),
            _const_spec((1, d)),
            _const_spec(win_b.shape),
            pl.BlockSpec((1, 1, QB_SWA), lambda b, j: (b, 0, j)),
            _const_spec((B_ROPE_HALF, QB_SWA)),
            _const_spec((B_HD, QB_SWA)),
            pl.BlockSpec((1, QB_SWA, B_KV_HEADS * B_K_PAD), lambda b, j: (b, j, 0)),
            pl.BlockSpec((1, B_WINDOW, B_KV_HEADS * B_K_PAD), lambda b, j: (b, prev_blk(b, j), 0)),
            pl.BlockSpec((1, B_KV_HEADS, B_V_PAD, QB_SWA), lambda b, j: (b, 0, 0, j)),
            pl.BlockSpec((1, B_KV_HEADS, B_V_PAD, B_WINDOW), lambda b, j: (b, 0, 0, prev_blk(b, j))),
        ],
        out_specs=pl.BlockSpec((QB_SWA, q_dim), lambda b, j: (b * ns_sw + j, 0)),
        out_shape=jax.ShapeDtypeStruct((t, q_dim), BF16),
        scratch_shapes=[pltpu.VMEM((QB_SWA, d), BF16),
                        pltpu.VMEM((blk_per_step, q_dim, B_WINDOW), BF16),
                        pltpu.VMEM((QB_SWA, q_dim), BF16),
                        pltpu.VMEM((blk_per_step, q_dim, B_WINDOW), F32)],
        compiler_params=_params(2),
        name="swa_attn",
    )(b_sinks[0], x1, b_norm[0].reshape(1, d), win_b, pos_row,
      jnp.broadcast_to(inv_b[:, None], (B_ROPE_HALF, QB_SWA)),
      jnp.broadcast_to(b_q_head_norm[0][:, None], (B_HD, QB_SWA)),
      k_b, k_b, vt_b, vt_b)

    n_out = t // TM_OUT
    out = pl.pallas_call(
        _swa_out_kernel,
        grid=(n_out,),
        in_specs=[
            pl.BlockSpec((TM_OUT, q_dim), lambda i: (i, 0)),
            pl.BlockSpec((TM_OUT, d), lambda i: (i, 0)),
            _const_spec(wo_b.shape),
        ],
        out_specs=pl.BlockSpec((TM_OUT, d), lambda i: (i, 0)),
        out_shape=jax.ShapeDtypeStruct((t, d), F32),
        compiler_params=_params(1),
        name="swa_out",
    )(og_b, x1, wo_b)
    return out.reshape(bsz, seq, d)
```

```python
import functools
import math

import jax
import jax.numpy as jnp
from jax import lax
from jax.experimental import pallas as pl
from jax.experimental.pallas import tpu as pltpu

F32 = jnp.float32
BF16 = jnp.bfloat16

EPS = 1e-6
ROPE_THETA = 500000.0
NEG = -1e30
LOG2E = math.log2(math.e)

A_HEADS = 16
A_LORA = 512
A_NOPE = 128
A_ROPE = 64
A_QK = A_NOPE + A_ROPE
A_QK_PAD = 256
A_V = 128
A_V_PAD = A_V + 16

B_HEADS = 32
B_KV_HEADS = 8
B_GROUP = B_HEADS // B_KV_HEADS
B_HD = 64
B_ROPE_HALF = 8
B_WINDOW = 128
B_K_PAD = 128
B_V_PAD = B_HD + 16

VMEM_LIMIT = 56 * 1024 * 1024

TM_UP = 256
TQ_A = 1024
TK_A = 512
TC_A = 256
TM_OUT = 512
QB_SWA = 512


def _const_spec(shape):
    return pl.BlockSpec(shape, lambda *_: (0,) * len(shape), pipeline_mode=pl.Buffered(1))


def _params(n_axes):
    return pltpu.CompilerParams(
        dimension_semantics=("arbitrary",) * n_axes, vmem_limit_bytes=VMEM_LIMIT)


def _rms_rows(xf, g_row):
    ms = jnp.mean(xf * xf, axis=-1, keepdims=True)
    return xf * lax.rsqrt(ms + EPS) * g_row


def _dot(a, b):
    return jnp.dot(a, b, preferred_element_type=F32)


def _dot_nt(a, b):
    return lax.dot_general(a, b, (((1,), (1,)), ((), ())), preferred_element_type=F32)


def _silu(g):
    return g * jax.nn.sigmoid(g)


def _mla_proj_kernel(x_ref, g_ref, wl_ref, wg_ref, glq_ref, glkv_ref, posr_ref, invc_ref,
                     wqt_ref, wk_ref, wvt_ref, gqc_ref, gkn_ref, gkr_ref,
                     qt_ref, k_ref, vt_ref, gate_ref, *, q_scale):
    tm = x_ref.shape[0]
    h = _rms_rows(x_ref[...], g_ref[...]).astype(BF16)
    lat = _dot(h, wl_ref[...])
    cq = _rms_rows(lat[:, :A_LORA], glq_ref[...]).astype(BF16)
    ckv = _rms_rows(lat[:, A_LORA:2 * A_LORA], glkv_ref[...]).astype(BF16)
    kr = lat[:, 2 * A_LORA:]

    ang_t = invc_ref[...] * posr_ref[0]
    cos_t = jnp.cos(ang_t)
    sin_t = jnp.sin(ang_t)
    cos_c = cos_t[None]
    sin_c = sin_t[None]
    gqc = gqc_ref[...][None]
    hpc = 4
    n_chunk = A_HEADS // hpc
    g_cols = gate_ref.shape[1] // n_chunk
    half = A_ROPE // 2
    for c in range(n_chunk):
        rows = slice(c * hpc * A_QK, (c + 1) * hpc * A_QK)
        qt = _dot_nt(wqt_ref[rows, :], cq).reshape(hpc, A_QK, tm)
        ss = jnp.sum(qt * qt, axis=1, keepdims=True)
        qn = qt * (lax.rsqrt(ss * (1.0 / A_QK) + EPS) * q_scale) * gqc
        x1 = qn[:, A_NOPE:A_NOPE + half]
        x2 = qn[:, A_NOPE + half:]
        full = jnp.concatenate(
            [qn[:, :A_NOPE], x1 * cos_c - x2 * sin_c, x2 * cos_c + x1 * sin_c,
             jnp.zeros((hpc, A_QK_PAD - A_QK, tm), F32)], axis=1)
        qt_ref[0, c * hpc:(c + 1) * hpc, 0] = full.astype(BF16)
        sl = slice(c * g_cols, (c + 1) * g_cols)
        gate_ref[:, sl] = _silu(_dot(h, wg_ref[:, sl])).astype(BF16)

    ss_r = jnp.sum(kr * kr, axis=-1, keepdims=True)
    krt = (kr * gkr_ref[...]).T
    y1 = krt[:half]
    y2 = krt[half:A_ROPE]
    kr_rot = jnp.concatenate(
        [y1 * cos_t - y2 * sin_t, y2 * cos_t + y1 * sin_t, krt[A_ROPE:]], axis=0).T
    gkn = gkn_ref[...]
    kn_all = _dot(ckv, wk_ref[...])
    for hh in range(A_HEADS):
        kn = kn_all[:, hh * A_NOPE:(hh + 1) * A_NOPE]
        ss = jnp.sum(kn * kn, axis=-1, keepdims=True) + ss_r
        r = lax.rsqrt(ss * (1.0 / A_QK) + EPS)
        k_ref[0, hh, :, :A_NOPE] = (kn * gkn * r).astype(BF16)
        k_ref[0, hh, :, A_NOPE:] = (kr_rot * r).astype(BF16)

    for c in range(n_chunk):
        rows = slice(c * hpc * A_V, (c + 1) * hpc * A_V)
        vt = _dot_nt(wvt_ref[rows, :], ckv)
        vt_ref[0, c * hpc:(c + 1) * hpc, 0, :A_V, :] = vt.reshape(hpc, A_V, tm).astype(BF16)
    ones_row = lax.broadcasted_iota(jnp.int32, (A_HEADS, A_V_PAD - A_V, tm), 1) == 0
    vt_ref[0, :, 0, A_V:, :] = jnp.where(ones_row, 1.0, 0.0).astype(BF16)


def _mla_attn_kernel(qt_ref, k_ref, vt_ref, gate_ref, o_ref,
                     s0_ref, s1_ref, acc_ref, m_ref, *, tq, tk, tc):
    seq = k_ref.shape[2]
    tv = vt_ref.shape[4]
    assert tq == 2 * tk and qt_ref.shape[4] == tc and tk % tv == 0
    n_c = tq // tc

    def scores(s_ref, ks, qs, c):
        s_ref[c] = _dot(k_ref[0, 0, pl.ds(ks, tk), :], qt_ref[0, 0, qs // tc + c])

    def softmax_pv(s_ref, ks, c, mask):
        st = s_ref[c]
        if mask is not None:
            st = jnp.where(mask, st, NEG)
        m = m_ref[c]
        m_new = jnp.maximum(m, jnp.max(st, axis=0, keepdims=True))
        p = jnp.exp2(st - m_new).astype(BF16)
        alpha = jnp.exp2(m - m_new)
        m_ref[c] = m_new
        vt = jnp.concatenate([vt_ref[0, 0, ks // tv + i] for i in range(tk // tv)], axis=1)
        acc_ref[c] = alpha * acc_ref[c] + _dot(vt, p)

    def q_body(qi, _):
        qs = pl.multiple_of(qi * tq, tq)

        def pair(j, _):
            ks = pl.multiple_of(j * tq, tq)
            for c in range(n_c):
                scores(s1_ref, ks + tk, qs, c)
                softmax_pv(s0_ref, ks, c, None)
            for c in range(n_c):
                scores(s0_ref, ks + tq, qs, c)
                softmax_pv(s1_ref, ks + tk, c, None)
            return 0

        def finish(c):
            o = (acc_ref[c, :A_V] * (1.0 / acc_ref[c, A_V:A_V + 1])).T
            rows = pl.ds(qs + c * tc, tc)
            o_ref[rows, :] = (o * gate_ref[rows, :].astype(F32)).astype(BF16)

        acc_ref[...] = jnp.zeros_like(acc_ref)
        m_ref[...] = jnp.full_like(m_ref, NEG)
        def four_pairs(j4, _):
            for i in range(4):
                pair(4 * j4 + i, 0)
            return 0

        lax.fori_loop(0, lax.shift_right_logical(qi, 2), four_pairs, 0)
        n_rest = qi & 3

        def diagonal():
            row = lax.broadcasted_iota(jnp.int32, (tk, tc), 0)
            col = lax.broadcasted_iota(jnp.int32, (tk, tc), 1)
            for c in range(n_c):
                if (c + 1) * tc > tk:
                    scores(s1_ref, qs + tk, qs, c)
            qs_next = pl.multiple_of(jnp.minimum(qs + tq, seq - tq), tq)
            n_late = 2
            for c in range(n_c):
                softmax_pv(s0_ref, qs, c, None if c * tc >= tk else row <= col + c * tc)
                if (c + 1) * tc <= tk:
                    finish(c)
                if c < n_c - n_late:
                    scores(s0_ref, 0, qs_next, c)
            for c in range(n_c):
                if (c + 1) * tc > tk:
                    softmax_pv(s1_ref, qs + tk, c, row + tk <= col + c * tc)
                    finish(c)
            for c in range(n_c - n_late, n_c):
                scores(s0_ref, 0, qs_next, c)

        for r in range(4):
            @pl.when(n_rest == r)
            def _(r=r):
                for i in range(r):
                    pair(qi - r + i, 0)
                diagonal()

        return 0

    for c in range(n_c):
        scores(s0_ref, 0, 0, c)
    lax.fori_loop(0, seq // tq, q_body, 0)


def _mla_out_kernel(og_ref, x_ref, wo_ref, gkv_ref, wkt_ref, wvt_ref, pos_ref, inv_ref, gk_ref,
                    x1_ref, k_ref, vt_ref):
    tm = x_ref.shape[0]
    x1 = x_ref[...] + _dot(og_ref[...], wo_ref[...].astype(BF16))
    x1_ref[...] = x1
    hk = _rms_rows(x1, gkv_ref[...]).astype(BF16)
    kt = _dot_nt(wkt_ref[...], hk).reshape(B_KV_HEADS, B_HD, tm)
    ss = jnp.sum(kt * kt, axis=1, keepdims=True)
    kn = kt * lax.rsqrt(ss * (1.0 / B_HD) + EPS) * gk_ref[...][None]
    ang = inv_ref[...] * pos_ref[0]
    c = jnp.cos(ang)[None]
    s = jnp.sin(ang)[None]
    x1r = kn[:, :B_ROPE_HALF]
    x2r = kn[:, B_ROPE_HALF:2 * B_ROPE_HALF]
    kfull = jnp.concatenate(
        [x1r * c - x2r * s, x2r * c + x1r * s, kn[:, 2 * B_ROPE_HALF:],
         jnp.zeros((B_KV_HEADS, B_K_PAD - B_HD, tm), F32)], axis=1)
    k_ref[0] = kfull.reshape(B_KV_HEADS * B_K_PAD, tm).T.astype(BF16)
    vt = _dot_nt(wvt_ref[...], hk).reshape(B_KV_HEADS, B_HD, tm)
    vt_ref[0, :, :B_HD, :] = vt.astype(BF16)
    ones_row = lax.broadcasted_iota(jnp.int32, (B_KV_HEADS, B_V_PAD - B_HD, tm), 1) == 0
    vt_ref[0, :, B_HD:, :] = jnp.where(ones_row, 1.0, 0.0).astype(BF16)


def _swa_attn_kernel(sink_ref, x1_ref, g_ref, win_ref, pos_ref, inv_ref, gq_ref,
                     kc_ref, kp_ref, vc_ref, vp_ref, o_ref,
                     hb_ref, qt_ref, gate_ref, ot_ref, *, q_scale):
    w = B_WINDOW
    tm = x1_ref.shape[0]
    n_blk = tm // w
    j = pl.program_id(1)
    gw = B_GROUP * w
    hb_ref[...] = _rms_rows(x1_ref[...], g_ref[...]).astype(BF16)

    ang = inv_ref[...] * pos_ref[0]
    cos_c = jnp.cos(ang)[None]
    sin_c = jnp.sin(ang)[None]
    gq = gq_ref[...][None]
    hpp = 8
    for p in range(B_HEADS // hpp):
        rows = slice(p * hpp * B_HD, (p + 1) * hpp * B_HD)
        qt = _dot(hb_ref[...], win_ref[:, rows]).T.reshape(hpp, B_HD, tm)
        ss = jnp.sum(qt * qt, axis=1, keepdims=True)
        qn = qt * (lax.rsqrt(ss * (1.0 / B_HD) + EPS) * q_scale) * gq
        x1r = qn[:, :B_ROPE_HALF]
        x2r = qn[:, B_ROPE_HALF:2 * B_ROPE_HALF]
        qfull = jnp.concatenate(
            [x1r * cos_c - x2r * sin_c, x2r * cos_c + x1r * sin_c, qn[:, 2 * B_ROPE_HALF:]], axis=1)
        qb = qfull.reshape(hpp * B_HD, tm).astype(BF16)
        for blk in range(n_blk):
            qt_ref[blk, rows, :] = qb[:, blk * w:(blk + 1) * w]

    g_rows = 2 * w
    g_cols = 256
    q_cols = B_HEADS * B_HD
    gate_pieces = [(r, c) for r in range(tm // g_rows) for c in range(gate_ref.shape[1] // g_cols)]

    def gate_piece(r, c):
        rs = slice(r * g_rows, (r + 1) * g_rows)
        cs = slice(c * g_cols, (c + 1) * g_cols)
        gate_ref[rs, cs] = _silu(
            _dot(hb_ref[rs, :], win_ref[:, q_cols + c * g_cols:q_cols + (c + 1) * g_cols])).astype(BF16)

    row = lax.broadcasted_iota(jnp.int32, (w, gw), 0)
    col = lax.broadcasted_iota(jnp.int32, (w, gw), 1)
    cur_visible = row <= (col & (w - 1))
    cur_visible_f = cur_visible.astype(F32)
    head_in_group = lax.broadcasted_iota(jnp.int32, (1, gw), 1) // w
    has_prev = j > 0

    def band_scores(blk, g):
        qs = slice(blk * w, (blk + 1) * w)
        ksl = slice(g * B_K_PAD, g * B_K_PAD + B_HD)
        if blk == 0:
            k_prev = kp_ref[0, :, ksl]
        else:
            k_prev = kc_ref[0, (blk - 1) * w:blk * w, ksl]
        k_band = jnp.concatenate([k_prev, kc_ref[0, qs, ksl]], axis=0)
        qt_g = jnp.concatenate(
            [qt_ref[blk, (B_GROUP * g + i) * B_HD:(B_GROUP * g + i + 1) * B_HD, :]
             for i in range(B_GROUP)], axis=1)
        return _dot(k_band, qt_g)

    def softmax_pv(st, blk, g):
        qs = slice(blk * w, (blk + 1) * w)
        s_prev = st[:w]
        if blk == 0:
            s_prev = jnp.where(has_prev, s_prev, NEG)
        s = jnp.where(cur_visible, st[w:], s_prev)
        sink = jnp.full((1, gw), sink_ref[B_GROUP * g] * LOG2E, F32)
        for i in range(1, B_GROUP):
            sink = jnp.where(head_in_group == i, sink_ref[B_GROUP * g + i] * LOG2E, sink)
        m = jnp.maximum(jnp.max(s, axis=0, keepdims=True), sink)
        e = jnp.exp2(s - m)
        e_cur = e * cur_visible_f
        p = jnp.concatenate([e - e_cur, e_cur], axis=0).astype(BF16)
        if blk == 0:
            v_prev = vp_ref[0, g]
        else:
            v_prev = vc_ref[0, g, :, (blk - 1) * w:blk * w]
        v_band = jnp.concatenate([v_prev, vc_ref[0, g, :, qs]], axis=1)
        pv = _dot(v_band, p)
        den = pv[B_HD:B_HD + 1] + jnp.exp2(sink - m)
        ot = pv[:B_HD] * (1.0 / den)
        for i in range(B_GROUP):
            hh = B_GROUP * g + i
            ot_ref[blk, hh * B_HD:(hh + 1) * B_HD, :] = ot[:, i * w:(i + 1) * w]

    def finish(blk):
        qs = slice(blk * w, (blk + 1) * w)
        o_ref[qs, :] = (ot_ref[blk].T * gate_ref[qs, :].astype(F32)).astype(BF16)

    order = [(blk, g) for blk in range(n_blk) for g in range(B_KV_HEADS)]
    assert len(order) == 2 * len(gate_pieces) and g_rows == 2 * w
    st = band_scores(*order[0])
    for idx, (blk, g) in enumerate(order):
        st_next = band_scores(*order[idx + 1]) if idx + 1 < len(order) else None
        softmax_pv(st, blk, g)
        if idx % 2 == 1:
            gate_piece(*gate_pieces[idx // 2])
        if g == B_KV_HEADS - 1 and blk % 2 == 1:
            finish(blk - 1)
            finish(blk)
        st = st_next


def _swa_out_kernel(og_ref, x1_ref, wo_ref, o_ref):
    o_ref[...] = x1_ref[...] + _dot(og_ref[...], wo_ref[...].astype(BF16))


def kernel(x, positions, a_norm, a_w_in, a_q_latent_norm, a_w_q_up, a_kv_latent_norm, a_w_kv_up,
           a_q_head_norm, a_k_head_norm, a_w_o, kv_norm, kv_w, kv_k_head_norm, b_norm, b_w_in,
           b_q_head_norm, b_sinks, b_w_o):
    bsz, seq, d = x.shape
    t = bsz * seq
    assert a_norm.shape[0] == 1 and b_norm.shape[0] == 1
    assert seq % TQ_A == 0 and seq % QB_SWA == 0 and TQ_A % TK_A == 0

    xf = x.reshape(t, d)
    pos_row = positions.reshape(bsz, 1, seq).astype(F32)

    w_in = a_w_in[0]
    n_lat = 2 * A_LORA + A_ROPE
    wl = jnp.pad(w_in[:, :n_lat], ((0, 0), (0, 128 - A_ROPE))).astype(BF16)
    wg = w_in[:, n_lat:].astype(BF16)
    wqt = a_w_q_up[0].astype(BF16).T
    wkv = a_w_kv_up[0].astype(BF16).reshape(A_LORA, A_HEADS, A_NOPE + A_V)
    wk = wkv[:, :, :A_NOPE].reshape(A_LORA, A_HEADS * A_NOPE)
    wvt = wkv[:, :, A_NOPE:].reshape(A_LORA, A_HEADS * A_V).T
    gq_col = jnp.broadcast_to(a_q_head_norm[0][:, None], (A_QK, TM_UP))
    gkn = a_k_head_norm[0][:A_NOPE].reshape(1, A_NOPE)
    gkr = jnp.pad(a_k_head_norm[0][A_NOPE:], (0, 128 - A_ROPE)).reshape(1, 128)
    inv_a = jnp.power(jnp.float32(ROPE_THETA), -jnp.arange(0, A_ROPE, 2, dtype=F32) / A_ROPE)
    inv_a_col = jnp.broadcast_to(inv_a[:, None], (A_ROPE // 2, TM_UP))
    wo_a = a_w_o[0]

    kv_dim = B_KV_HEADS * B_HD
    kv_w16 = kv_w.astype(BF16)
    wkt_b = kv_w16[:, :kv_dim].T
    wvt_b = kv_w16[:, kv_dim:].T
    rot_b = 2 * B_ROPE_HALF
    inv_b = jnp.power(jnp.float32(ROPE_THETA), -jnp.arange(0, rot_b, 2, dtype=F32) / rot_b)
    q_dim = B_HEADS * B_HD
    win_b = b_w_in[0].astype(BF16)
    wo_b = b_w_o[0]

    ns_up = seq // TM_UP
    tok_up = lambda b, j: (b * ns_up + j, 0)
    qt_a, k_a, vt_a, gate_a = pl.pallas_call(
        functools.partial(_mla_proj_kernel, q_scale=A_QK ** -0.5 * LOG2E),
        grid=(bsz, ns_up),
        in_specs=[
            pl.BlockSpec((TM_UP, d), tok_up),
            _const_spec((1, d)),
            _const_spec(wl.shape),
            _const_spec(wg.shape),
            _const_spec((1, A_LORA)),
            _const_spec((1, A_LORA)),
            pl.BlockSpec((1, 1, TM_UP), lambda b, j: (b, 0, j)),
            _const_spec(inv_a_col.shape),
            _const_spec(wqt.shape),
            _const_spec(wk.shape),
            _const_spec(wvt.shape),
            _const_spec(gq_col.shape),
            _const_spec((1, A_NOPE)),
            _const_spec((1, 128)),
        ],
        out_specs=[
            pl.BlockSpec((1, A_HEADS, 1, A_QK_PAD, TM_UP), lambda b, j: (b, 0, j, 0, 0)),
            pl.BlockSpec((1, A_HEADS, TM_UP, A_QK_PAD), lambda b, j: (b, 0, j, 0)),
            pl.BlockSpec((1, A_HEADS, 1, A_V_PAD, TM_UP), lambda b, j: (b, 0, j, 0, 0)),
            pl.BlockSpec((TM_UP, d), tok_up),
        ],
        out_shape=[
            jax.ShapeDtypeStruct((bsz, A_HEADS, ns_up, A_QK_PAD, TM_UP), BF16),
            jax.ShapeDtypeStruct((bsz, A_HEADS, seq, A_QK_PAD), BF16),
            jax.ShapeDtypeStruct((bsz, A_HEADS, ns_up, A_V_PAD, TM_UP), BF16),
            jax.ShapeDtypeStruct((t, A_HEADS * A_V), BF16),
        ],
        compiler_params=_params(2),
        name="mla_proj",
    )(xf, a_norm[0].reshape(1, d), wl, wg, a_q_latent_norm[0].reshape(1, A_LORA),
      a_kv_latent_norm[0].reshape(1, A_LORA), pos_row, inv_a_col, wqt, wk, wvt, gq_col, gkn, gkr)

    og_a = pl.pallas_call(
        functools.partial(_mla_attn_kernel, tq=TQ_A, tk=TK_A, tc=TC_A),
        grid=(bsz, A_HEADS),
        in_specs=[
            pl.BlockSpec((1, 1, ns_up, A_QK_PAD, TM_UP), lambda b, h: (b, h, 0, 0, 0)),
            pl.BlockSpec((1, 1, seq, A_QK_PAD), lambda b, h: (b, h, 0, 0)),
            pl.BlockSpec((1, 1, ns_up, A_V_PAD, TM_UP), lambda b, h: (b, h, 0, 0, 0)),
            pl.BlockSpec((seq, A_V), lambda b, h: (b, h)),
        ],
        out_specs=pl.BlockSpec((seq, A_V), lambda b, h: (b, h)),
        out_shape=jax.ShapeDtypeStruct((t, A_HEADS * A_V), BF16),
        scratch_shapes=[pltpu.VMEM((TQ_A // TC_A, TK_A, TC_A), F32), pltpu.VMEM((TQ_A // TC_A, TK_A, TC_A), F32),
                        pltpu.VMEM((TQ_A // TC_A, A_V_PAD, TC_A), F32), pltpu.VMEM((TQ_A // TC_A, 1, TC_A), F32)],
        compiler_params=_params(2),
        name="mla_attn",
    )(qt_a, k_a, vt_a, gate_a)

    ns_out = seq // TM_OUT
    tok_out = lambda b, j: (b * ns_out + j, 0)
    x1, k_b, vt_b = pl.pallas_call(
        _mla_out_kernel,
        grid=(bsz, ns_out),
        in_specs=[
            pl.BlockSpec((TM_OUT, d), tok_out),
            pl.BlockSpec((TM_OUT, d), tok_out),
            _const_spec(wo_a.shape),
            _const_spec((1, d)),
            _const_spec(wkt_b.shape),
            _const_spec(wvt_b.shape),
            pl.BlockSpec((1, 1, TM_OUT), lambda b, j: (b, 0, j)),
            _const_spec((B_ROPE_HALF, TM_OUT)),
            _const_spec((B_HD, TM_OUT)),
        ],
        out_specs=[
            pl.BlockSpec((TM_OUT, d), tok_out),
            pl.BlockSpec((1, TM_OUT, B_KV_HEADS * B_K_PAD), lambda b, j: (b, j, 0)),
            pl.BlockSpec((1, B_KV_HEADS, B_V_PAD, TM_OUT), lambda b, j: (b, 0, 0, j)),
        ],
        out_shape=[
            jax.ShapeDtypeStruct((t, d), F32),
            jax.ShapeDtypeStruct((bsz, seq, B_KV_HEADS * B_K_PAD), BF16),
            jax.ShapeDtypeStruct((bsz, B_KV_HEADS, B_V_PAD, seq), BF16),
        ],
        compiler_params=_params(2),
        name="mla_out",
    )(og_a, xf, wo_a, kv_norm.reshape(1, d), wkt_b, wvt_b, pos_row,
      jnp.broadcast_to(inv_b[:, None], (B_ROPE_HALF, TM_OUT)),
      jnp.broadcast_to(kv_k_head_norm[:, None], (B_HD, TM_OUT)))

    ns_sw = seq // QB_SWA
    blk_per_step = QB_SWA // B_WINDOW
    prev_blk = lambda b, j: jnp.maximum(j * blk_per_step - 1, 0)
    og_b = pl.pallas_call(
        functools.partial(_swa_attn_kernel, q_scale=B_HD ** -0.5 * LOG2E),
        grid=(bsz, ns_sw),
        in_specs=[
            pl.BlockSpec(memory_space=pltpu.SMEM),
            pl.BlockSpec((QB_SWA, d), lambda b, j: (b * ns_sw + j, 0)),
            _const_spec((1, d)),
            _const_spec(win_b.shape),
            pl.BlockSpec((1, 1, QB_SWA), lambda b, j: (b, 0, j)),
            _const_spec((B_ROPE_HALF, QB_SWA)),
            _const_spec((B_HD, QB_SWA)),
            pl.BlockSpec((1, QB_SWA, B_KV_HEADS * B_K_PAD), lambda b, j: (b, j, 0)),
            pl.BlockSpec((1, B_WINDOW, B_KV_HEADS * B_K_PAD), lambda b, j: (b, prev_blk(b, j), 0)),
            pl.BlockSpec((1, B_KV_HEADS, B_V_PAD, QB_SWA), lambda b, j: (b, 0, 0, j)),
            pl.BlockSpec((1, B_KV_HEADS, B_V_PAD, B_WINDOW), lambda b, j: (b, 0, 0, prev_blk(b, j))),
        ],
        out_specs=pl.BlockSpec((QB_SWA, q_dim), lambda b, j: (b * ns_sw + j, 0)),
        out_shape=jax.ShapeDtypeStruct((t, q_dim), BF16),
        scratch_shapes=[pltpu.VMEM((QB_SWA, d), BF16),
                        pltpu.VMEM((blk_per_step, q_dim, B_WINDOW), BF16),
                        pltpu.VMEM((QB_SWA, q_dim), BF16),
                        pltpu.VMEM((blk_per_step, q_dim, B_WINDOW), F32)],
        compiler_params=_params(2),
        name="swa_attn",
    )(b_sinks[0], x1, b_norm[0].reshape(1, d), win_b, pos_row,
      jnp.broadcast_to(inv_b[:, None], (B_ROPE_HALF, QB_SWA)),
      jnp.broadcast_to(b_q_head_norm[0][:, None], (B_HD, QB_SWA)),
      k_b, k_b, vt_b, vt_b)

    n_out = t // TM_OUT
    out = pl.pallas_call(
        _swa_out_kernel,
        grid=(n_out,),
        in_specs=[
            pl.BlockSpec((TM_OUT, q_dim), lambda i: (i, 0)),
            pl.BlockSpec((TM_OUT, d), lambda i: (i, 0)),
            _const_spec(wo_b.shape),
        ],
        out_specs=pl.BlockSpec((TM_OUT, d), lambda i: (i, 0)),
        out_shape=jax.ShapeDtypeStruct((t, d), F32),
        compiler_params=_params(1),
        name="swa_out",
    )(og_b, x1, wo_b)
    return out.reshape(bsz, seq, d)
```

```python
import functools
import math

import jax
import jax.numpy as jnp
from jax import lax
from jax.experimental import pallas as pl
from jax.experimental.pallas import tpu as pltpu

F32 = jnp.float32
BF16 = jnp.bfloat16

EPS = 1e-6
ROPE_THETA = 500000.0
NEG = -1e30
LOG2E = math.log2(math.e)

A_HEADS = 16
A_LORA = 512
A_NOPE = 128
A_ROPE = 64
A_QK = A_NOPE + A_ROPE
A_QK_PAD = 256
A_V = 128
A_V_PAD = A_V + 16

B_HEADS = 32
B_KV_HEADS = 8
B_GROUP = B_HEADS // B_KV_HEADS
B_HD = 64
B_ROPE_HALF = 8
B_WINDOW = 128
B_K_PAD = 128
B_V_PAD = B_HD + 16

VMEM_LIMIT = 56 * 1024 * 1024

TM_UP = 256
TQ_A = 1024
TK_A = 512
TC_A = 256
TM_OUT = 512
QB_SWA = 512


def _const_spec(shape):
    return pl.BlockSpec(shape, lambda *_: (0,) * len(shape), pipeline_mode=pl.Buffered(1))


def _params(n_axes):
    return pltpu.CompilerParams(
        dimension_semantics=("arbitrary",) * n_axes, vmem_limit_bytes=VMEM_LIMIT)


def _rms_rows(xf, g_row):
    ms = jnp.mean(xf * xf, axis=-1, keepdims=True)
    return xf * lax.rsqrt(ms + EPS) * g_row


def _dot(a, b):
    return jnp.dot(a, b, preferred_element_type=F32)


def _dot_nt(a, b):
    return lax.dot_general(a, b, (((1,), (1,)), ((), ())), preferred_element_type=F32)


def _silu(g):
    return g * jax.nn.sigmoid(g)


def _mla_proj_kernel(x_ref, g_ref, wl_ref, wg_ref, glq_ref, glkv_ref, posr_ref, invc_ref,
                     wqt_ref, wk_ref, wvt_ref, gqc_ref, gkn_ref, gkr_ref,
                     qt_ref, k_ref, vt_ref, gate_ref, *, q_scale):
    tm = x_ref.shape[0]
    h = _rms_rows(x_ref[...], g_ref[...]).astype(BF16)
    lat = _dot(h, wl_ref[...])
    cq = _rms_rows(lat[:, :A_LORA], glq_ref[...]).astype(BF16)
    ckv = _rms_rows(lat[:, A_LORA:2 * A_LORA], glkv_ref[...]).astype(BF16)
    kr = lat[:, 2 * A_LORA:]

    ang_t = invc_ref[...] * posr_ref[0]
    cos_t = jnp.cos(ang_t)
    sin_t = jnp.sin(ang_t)
    cos_c = cos_t[None]
    sin_c = sin_t[None]
    gqc = gqc_ref[...][None]
    hpc = 4
    n_chunk = A_HEADS // hpc
    g_cols = gate_ref.shape[1] // n_chunk
    half = A_ROPE // 2
    for c in range(n_chunk):
        rows = slice(c * hpc * A_QK, (c + 1) * hpc * A_QK)
        qt = _dot_nt(wqt_ref[rows, :], cq).reshape(hpc, A_QK, tm)
        ss = jnp.sum(qt * qt, axis=1, keepdims=True)
        qn = qt * (lax.rsqrt(ss * (1.0 / A_QK) + EPS) * q_scale) * gqc
        x1 = qn[:, A_NOPE:A_NOPE + half]
        x2 = qn[:, A_NOPE + half:]
        full = jnp.concatenate(
            [qn[:, :A_NOPE], x1 * cos_c - x2 * sin_c, x2 * cos_c + x1 * sin_c,
             jnp.zeros((hpc, A_QK_PAD - A_QK, tm), F32)], axis=1)
        qt_ref[0, c * hpc:(c + 1) * hpc, 0] = full.astype(BF16)
        sl = slice(c * g_cols, (c + 1) * g_cols)
        gate_ref[:, sl] = _silu(_dot(h, wg_ref[:, sl])).astype(BF16)

    ss_r = jnp.sum(kr * kr, axis=-1, keepdims=True)
    krt = (kr * gkr_ref[...]).T
    y1 = krt[:half]
    y2 = krt[half:A_ROPE]
    kr_rot = jnp.concatenate(
        [y1 * cos_t - y2 * sin_t, y2 * cos_t + y1 * sin_t, krt[A_ROPE:]], axis=0).T
    gkn = gkn_ref[...]
    kn_all = _dot(ckv, wk_ref[...])
    for hh in range(A_HEADS):
        kn = kn_all[:, hh * A_NOPE:(hh + 1) * A_NOPE]
        ss = jnp.sum(kn * kn, axis=-1, keepdims=True) + ss_r
        r = lax.rsqrt(ss * (1.0 / A_QK) + EPS)
        k_ref[0, hh, :, :A_NOPE] = (kn * gkn * r).astype(BF16)
        k_ref[0, hh, :, A_NOPE:] = (kr_rot * r).astype(BF16)

    for c in range(n_chunk):
        rows = slice(c * hpc * A_V, (c + 1) * hpc * A_V)
        vt = _dot_nt(wvt_ref[rows, :], ckv)
        vt_ref[0, c * hpc:(c + 1) * hpc, 0, :A_V, :] = vt.reshape(hpc, A_V, tm).astype(BF16)
    ones_row = lax.broadcasted_iota(jnp.int32, (A_HEADS, A_V_PAD - A_V, tm), 1) == 0
    vt_ref[0, :, 0, A_V:, :] = jnp.where(ones_row, 1.0, 0.0).astype(BF16)


def _mla_attn_kernel(qt_ref, k_ref, vt_ref, gate_ref, o_ref,
                     s0_ref, s1_ref, acc_ref, m_ref, *, tq, tk, tc):
    seq = k_ref.shape[2]
    tv = vt_ref.shape[4]
    assert tq == 2 * tk and qt_ref.shape[4] == tc and tk % tv == 0
    n_c = tq // tc

    def scores(s_ref, ks, qs, c):
        s_ref[c] = _dot(k_ref[0, 0, pl.ds(ks, tk), :], qt_ref[0, 0, qs // tc + c])

    def softmax_pv(s_ref, ks, c, mask):
        st = s_ref[c]
        if mask is not None:
            st = jnp.where(mask, st, NEG)
        m = m_ref[c]
        m_new = jnp.maximum(m, jnp.max(st, axis=0, keepdims=True))
        p = jnp.exp2(st - m_new).astype(BF16)
        alpha = jnp.exp2(m - m_new)
        m_ref[c] = m_new
        vt = jnp.concatenate([vt_ref[0, 0, ks // tv + i] for i in range(tk // tv)], axis=1)
        acc_ref[c] = alpha * acc_ref[c] + _dot(vt, p)

    def q_body(qi, _):
        qs = pl.multiple_of(qi * tq, tq)

        def pair(j, _):
            ks = pl.multiple_of(j * tq, tq)
            for c in range(n_c):
                scores(s1_ref, ks + tk, qs, c)
                softmax_pv(s0_ref, ks, c, None)
            for c in range(n_c):
                scores(s0_ref, ks + tq, qs, c)
                softmax_pv(s1_ref, ks + tk, c, None)
            return 0

        def finish(c):
            o = (acc_ref[c, :A_V] * (1.0 / acc_ref[c, A_V:A_V + 1])).T
            rows = pl.ds(qs + c * tc, tc)
            o_ref[rows, :] = (o * gate_ref[rows, :].astype(F32)).astype(BF16)

        acc_ref[...] = jnp.zeros_like(acc_ref)
        m_ref[...] = jnp.full_like(m_ref, NEG)
        def four_pairs(j4, _):
            for i in range(4):
                pair(4 * j4 + i, 0)
            return 0

        n_four = lax.shift_right_logical(qi, 2)
        n_rest = qi & 3
        join_last = jnp.where((n_rest >= 2) & (n_four >= 1), 1, 0)
        lax.fori_loop(0, n_four - join_last, four_pairs, 0)

        def diagonal():
            row = lax.broadcasted_iota(jnp.int32, (tk, tc), 0)
            col = lax.broadcasted_iota(jnp.int32, (tk, tc), 1)
            for c in range(n_c):
                if (c + 1) * tc > tk:
                    scores(s1_ref, qs + tk, qs, c)
            qs_next = pl.multiple_of(jnp.minimum(qs + tq, seq - tq), tq)
            n_late = 2
            for c in range(n_c):
                softmax_pv(s0_ref, qs, c, None if c * tc >= tk else row <= col + c * tc)
                if (c + 1) * tc <= tk:
                    finish(c)
                if c < n_c - n_late:
                    scores(s0_ref, 0, qs_next, c)
            for c in range(n_c):
                if (c + 1) * tc > tk:
                    softmax_pv(s1_ref, qs + tk, c, row + tk <= col + c * tc)
                    finish(c)
            for c in range(n_c - n_late, n_c):
                scores(s0_ref, 0, qs_next, c)

        for r in range(4):
            for joined in ((0, 1) if r >= 2 else (0,)):
                cond = (n_rest == r) & (join_last == joined) if r >= 2 else n_rest == r

                @pl.when(cond)
                def _(n_tail=r + 4 * joined):
                    for i in range(n_tail):
                        pair(qi - n_tail + i, 0)
                    diagonal()

        return 0

    for c in range(n_c):
        scores(s0_ref, 0, 0, c)
    lax.fori_loop(0, seq // tq, q_body, 0)


def _mla_out_kernel(og_ref, x_ref, wo_ref, gkv_ref, wkt_ref, wvt_ref, pos_ref, inv_ref, gk_ref,
                    x1_ref, k_ref, vt_ref):
    tm = x_ref.shape[0]
    x1 = x_ref[...] + _dot(og_ref[...], wo_ref[...].astype(BF16))
    x1_ref[...] = x1
    hk = _rms_rows(x1, gkv_ref[...]).astype(BF16)
    kt = _dot_nt(wkt_ref[...], hk).reshape(B_KV_HEADS, B_HD, tm)
    ss = jnp.sum(kt * kt, axis=1, keepdims=True)
    kn = kt * lax.rsqrt(ss * (1.0 / B_HD) + EPS) * gk_ref[...][None]
    ang = inv_ref[...] * pos_ref[0]
    c = jnp.cos(ang)[None]
    s = jnp.sin(ang)[None]
    x1r = kn[:, :B_ROPE_HALF]
    x2r = kn[:, B_ROPE_HALF:2 * B_ROPE_HALF]
    kfull = jnp.concatenate(
        [x1r * c - x2r * s, x2r * c + x1r * s, kn[:, 2 * B_ROPE_HALF:],
         jnp.zeros((B_KV_HEADS, B_K_PAD - B_HD, tm), F32)], axis=1)
    k_ref[0] = kfull.reshape(B_KV_HEADS * B_K_PAD, tm).T.astype(BF16)
    vt = _dot_nt(wvt_ref[...], hk).reshape(B_KV_HEADS, B_HD, tm)
    vt_ref[0, :, :B_HD, :] = vt.astype(BF16)
    ones_row = lax.broadcasted_iota(jnp.int32, (B_KV_HEADS, B_V_PAD - B_HD, tm), 1) == 0
    vt_ref[0, :, B_HD:, :] = jnp.where(ones_row, 1.0, 0.0).astype(BF16)


def _swa_attn_kernel(sink_ref, x1_ref, g_ref, win_ref, pos_ref, inv_ref, gq_ref,
                     kc_ref, kp_ref, vc_ref, vp_ref, o_ref,
                     hb_ref, qt_ref, gate_ref, ot_ref, *, q_scale):
    w = B_WINDOW
    tm = x1_ref.shape[0]
    n_blk = tm // w
    j = pl.program_id(1)
    gw = B_GROUP * w
    hb_ref[...] = _rms_rows(x1_ref[...], g_ref[...]).astype(BF16)

    ang = inv_ref[...] * pos_ref[0]
    cos_c = jnp.cos(ang)[None]
    sin_c = jnp.sin(ang)[None]
    gq = gq_ref[...][None]
    hpp = 8
    for p in range(B_HEADS // hpp):
        rows = slice(p * hpp * B_HD, (p + 1) * hpp * B_HD)
        qt = _dot(hb_ref[...], win_ref[:, rows]).T.reshape(hpp, B_HD, tm)
        ss = jnp.sum(qt * qt, axis=1, keepdims=True)
        qn = qt * (lax.rsqrt(ss * (1.0 / B_HD) + EPS) * q_scale) * gq
        x1r = qn[:, :B_ROPE_HALF]
        x2r = qn[:, B_ROPE_HALF:2 * B_ROPE_HALF]
        qfull = jnp.concatenate(
            [x1r * cos_c - x2r * sin_c, x2r * cos_c + x1r * sin_c, qn[:, 2 * B_ROPE_HALF:]], axis=1)
        qb = qfull.reshape(hpp * B_HD, tm).astype(BF16)
        for blk in range(n_blk):
            qt_ref[blk, rows, :] = qb[:, blk * w:(blk + 1) * w]

    g_rows = 2 * w
    g_cols = 256
    q_cols = B_HEADS * B_HD
    gate_pieces = [(r, c) for r in range(tm // g_rows) for c in range(gate_ref.shape[1] // g_cols)]

    def gate_piece(r, c):
        rs = slice(r * g_rows, (r + 1) * g_rows)
        cs = slice(c * g_cols, (c + 1) * g_cols)
        gate_ref[rs, cs] = _silu(
            _dot(hb_ref[rs, :], win_ref[:, q_cols + c * g_cols:q_cols + (c + 1) * g_cols])).astype(BF16)

    row = lax.broadcasted_iota(jnp.int32, (w, gw), 0)
    col = lax.broadcasted_iota(jnp.int32, (w, gw), 1)
    cur_visible = row <= (col & (w - 1))
    cur_visible_f = cur_visible.astype(F32)
    head_in_group = lax.broadcasted_iota(jnp.int32, (1, gw), 1) // w
    has_prev = j > 0

    def band_scores(blk, g):
        qs = slice(blk * w, (blk + 1) * w)
        ksl = slice(g * B_K_PAD, g * B_K_PAD + B_HD)
        if blk == 0:
            k_prev = kp_ref[0, :, ksl]
        else:
            k_prev = kc_ref[0, (blk - 1) * w:blk * w, ksl]
        k_band = jnp.concatenate([k_prev, kc_ref[0, qs, ksl]], axis=0)
        qt_g = jnp.concatenate(
            [qt_ref[blk, (B_GROUP * g + i) * B_HD:(B_GROUP * g + i + 1) * B_HD, :]
             for i in range(B_GROUP)], axis=1)
        return _dot(k_band, qt_g)

    def softmax_pv(st, blk, g):
        qs = slice(blk * w, (blk + 1) * w)
        s_prev = st[:w]
        if blk == 0:
            s_prev = jnp.where(has_prev, s_prev, NEG)
        s = jnp.where(cur_visible, st[w:], s_prev)
        sink = jnp.full((1, gw), sink_ref[B_GROUP * g] * LOG2E, F32)
        for i in range(1, B_GROUP):
            sink = jnp.where(head_in_group == i, sink_ref[B_GROUP * g + i] * LOG2E, sink)
        m = jnp.maximum(jnp.max(s, axis=0, keepdims=True), sink)
        e = jnp.exp2(s - m)
        e_cur = e * cur_visible_f
        p = jnp.concatenate([e - e_cur, e_cur], axis=0).astype(BF16)
        if blk == 0:
            v_prev = vp_ref[0, g]
        else:
            v_prev = vc_ref[0, g, :, (blk - 1) * w:blk * w]
        v_band = jnp.concatenate([v_prev, vc_ref[0, g, :, qs]], axis=1)
        pv = _dot(v_band, p)
        den = pv[B_HD:B_HD + 1] + jnp.exp2(sink - m)
        ot = pv[:B_HD] * (1.0 / den)
        for i in range(B_GROUP):
            hh = B_GROUP * g + i
            ot_ref[blk, hh * B_HD:(hh + 1) * B_HD, :] = ot[:, i * w:(i + 1) * w]

    def finish(blk):
        qs = slice(blk * w, (blk + 1) * w)
        o_ref[qs, :] = (ot_ref[blk].T * gate_ref[qs, :].astype(F32)).astype(BF16)

    order = [(blk, g) for blk in range(n_blk) for g in range(B_KV_HEADS)]
    assert len(order) == 2 * len(gate_pieces) and g_rows == 2 * w
    st = band_scores(*order[0])
    for idx, (blk, g) in enumerate(order):
        st_next = band_scores(*order[idx + 1]) if idx + 1 < len(order) else None
        softmax_pv(st, blk, g)
        if idx % 2 == 1:
            gate_piece(*gate_pieces[idx // 2])
        if g == B_KV_HEADS - 1 and blk % 2 == 1:
            finish(blk - 1)
            finish(blk)
        st = st_next


def _swa_out_kernel(og_ref, x1_ref, wo_ref, o_ref):
    o_ref[...] = x1_ref[...] + _dot(og_ref[...], wo_ref[...].astype(BF16))


def kernel(x, positions, a_norm, a_w_in, a_q_latent_norm, a_w_q_up, a_kv_latent_norm, a_w_kv_up,
           a_q_head_norm, a_k_head_norm, a_w_o, kv_norm, kv_w, kv_k_head_norm, b_norm, b_w_in,
           b_q_head_norm, b_sinks, b_w_o):
    bsz, seq, d = x.shape
    t = bsz * seq
    assert a_norm.shape[0] == 1 and b_norm.shape[0] == 1
    assert seq % TQ_A == 0 and seq % QB_SWA == 0 and TQ_A % TK_A == 0

    xf = x.reshape(t, d)
    pos_row = positions.reshape(bsz, 1, seq).astype(F32)

    w_in = a_w_in[0]
    n_lat = 2 * A_LORA + A_ROPE
    wl = jnp.pad(w_in[:, :n_lat], ((0, 0), (0, 128 - A_ROPE))).astype(BF16)
    wg = w_in[:, n_lat:].astype(BF16)
    wqt = a_w_q_up[0].astype(BF16).T
    wkv = a_w_kv_up[0].astype(BF16).reshape(A_LORA, A_HEADS, A_NOPE + A_V)
    wk = wkv[:, :, :A_NOPE].reshape(A_LORA, A_HEADS * A_NOPE)
    wvt = wkv[:, :, A_NOPE:].reshape(A_LORA, A_HEADS * A_V).T
    gq_col = jnp.broadcast_to(a_q_head_norm[0][:, None], (A_QK, TM_UP))
    gkn = a_k_head_norm[0][:A_NOPE].reshape(1, A_NOPE)
    gkr = jnp.pad(a_k_head_norm[0][A_NOPE:], (0, 128 - A_ROPE)).reshape(1, 128)
    inv_a = jnp.power(jnp.float32(ROPE_THETA), -jnp.arange(0, A_ROPE, 2, dtype=F32) / A_ROPE)
    inv_a_col = jnp.broadcast_to(inv_a[:, None], (A_ROPE // 2, TM_UP))
    wo_a = a_w_o[0]

    kv_dim = B_KV_HEADS * B_HD
    kv_w16 = kv_w.astype(BF16)
    wkt_b = kv_w16[:, :kv_dim].T
    wvt_b = kv_w16[:, kv_dim:].T
    rot_b = 2 * B_ROPE_HALF
    inv_b = jnp.power(jnp.float32(ROPE_THETA), -jnp.arange(0, rot_b, 2, dtype=F32) / rot_b)
    q_dim = B_HEADS * B_HD
    win_b = b_w_in[0].astype(BF16)
    wo_b = b_w_o[0]

    ns_up = seq // TM_UP
    tok_up = lambda b, j: (b * ns_up + j, 0)
    qt_a, k_a, vt_a, gate_a = pl.pallas_call(
        functools.partial(_mla_proj_kernel, q_scale=A_QK ** -0.5 * LOG2E),
        grid=(bsz, ns_up),
        in_specs=[
            pl.BlockSpec((TM_UP, d), tok_up),
            _const_spec((1, d)),
            _const_spec(wl.shape),
            _const_spec(wg.shape),
            _const_spec((1, A_LORA)),
            _const_spec((1, A_LORA)),
            pl.BlockSpec((1, 1, TM_UP), lambda b, j: (b, 0, j)),
            _const_spec(inv_a_col.shape),
            _const_spec(wqt.shape),
            _const_spec(wk.shape),
            _const_spec(wvt.shape),
            _const_spec(gq_col.shape),
            _const_spec((1, A_NOPE)),
            _const_spec((1, 128)),
        ],
        out_specs=[
            pl.BlockSpec((1, A_HEADS, 1, A_QK_PAD, TM_UP), lambda b, j: (b, 0, j, 0, 0)),
            pl.BlockSpec((1, A_HEADS, TM_UP, A_QK_PAD), lambda b, j: (b, 0, j, 0)),
            pl.BlockSpec((1, A_HEADS, 1, A_V_PAD, TM_UP), lambda b, j: (b, 0, j, 0, 0)),
            pl.BlockSpec((TM_UP, d), tok_up),
        ],
        out_shape=[
            jax.ShapeDtypeStruct((bsz, A_HEADS, ns_up, A_QK_PAD, TM_UP), BF16),
            jax.ShapeDtypeStruct((bsz, A_HEADS, seq, A_QK_PAD), BF16),
            jax.ShapeDtypeStruct((bsz, A_HEADS, ns_up, A_V_PAD, TM_UP), BF16),
            jax.ShapeDtypeStruct((t, A_HEADS * A_V), BF16),
        ],
        compiler_params=_params(2),
        name="mla_proj",
    )(xf, a_norm[0].reshape(1, d), wl, wg, a_q_latent_norm[0].reshape(1, A_LORA),
      a_kv_latent_norm[0].reshape(1, A_LORA), pos_row, inv_a_col, wqt, wk, wvt, gq_col, gkn, gkr)

    og_a = pl.pallas_call(
        functools.partial(_mla_attn_kernel, tq=TQ_A, tk=TK_A, tc=TC_A),
        grid=(bsz, A_HEADS),
        in_specs=[
            pl.BlockSpec((1, 1, ns_up, A_QK_PAD, TM_UP), lambda b, h: (b, h, 0, 0, 0)),
            pl.BlockSpec((1, 1, seq, A_QK_PAD), lambda b, h: (b, h, 0, 0)),
            pl.BlockSpec((1, 1, ns_up, A_V_PAD, TM_UP), lambda b, h: (b, h, 0, 0, 0)),
            pl.BlockSpec((seq, A_V), lambda b, h: (b, h)),
        ],
        out_specs=pl.BlockSpec((seq, A_V), lambda b, h: (b, h)),
        out_shape=jax.ShapeDtypeStruct((t, A_HEADS * A_V), BF16),
        scratch_shapes=[pltpu.VMEM((TQ_A // TC_A, TK_A, TC_A), F32), pltpu.VMEM((TQ_A // TC_A, TK_A, TC_A), F32),
                        pltpu.VMEM((TQ_A // TC_A, A_V_PAD, TC_A), F32), pltpu.VMEM((TQ_A // TC_A, 1, TC_A), F32)],
        compiler_params=_params(2),
        name="mla_attn",
    )(qt_a, k_a, vt_a, gate_a)

    ns_out = seq // TM_OUT
    tok_out = lambda b, j: (b * ns_out + j, 0)
    x1, k_b, vt_b = pl.pallas_call(
        _mla_out_kernel,
        grid=(bsz, ns_out),
        in_specs=[
            pl.BlockSpec((TM_OUT, d), tok_out),
            pl.BlockSpec((TM_OUT, d), tok_out),
            _const_spec(wo_a.shape),
            _const_spec((1, d)),
            _const_spec(wkt_b.shape),
            _const_spec(wvt_b.shape),
            pl.BlockSpec((1, 1, TM_OUT), lambda b, j: (b, 0, j)),
            _const_spec((B_ROPE_HALF, TM_OUT)),
            _const_spec((B_HD, TM_OUT)),
        ],
        out_specs=[
            pl.BlockSpec((TM_OUT, d), tok_out),
            pl.BlockSpec((1, TM_OUT, B_KV_HEADS * B_K_PAD), lambda b, j: (b, j, 0)),
            pl.BlockSpec((1, B_KV_HEADS, B_V_PAD, TM_OUT), lambda b, j: (b, 0, 0, j)),
        ],
        out_shape=[
            jax.ShapeDtypeStruct((t, d), F32),
            jax.ShapeDtypeStruct((bsz, seq, B_KV_HEADS * B_K_PAD), BF16),
            jax.ShapeDtypeStruct((bsz, B_KV_HEADS, B_V_PAD, seq), BF16),
        ],
        compiler_params=_params(2),
        name="mla_out",
    )(og_a, xf, wo_a, kv_norm.reshape(1, d), wkt_b, wvt_b, pos_row,
      jnp.broadcast_to(inv_b[:, None], (B_ROPE_HALF, TM_OUT)),
      jnp.broadcast_to(kv_k_head_norm[:, None], (B_HD, TM_OUT)))

    ns_sw = seq // QB_SWA
    blk_per_step = QB_SWA // B_WINDOW
    prev_blk = lambda b, j: jnp.maximum(j * blk_per_step - 1, 0)
    og_b = pl.pallas_call(
        functools.partial(_swa_attn_kernel, q_scale=B_HD ** -0.5 * LOG2E),
        grid=(bsz, ns_sw),
        in_specs=[
            pl.BlockSpec(memory_space=pltpu.SMEM),
            pl.BlockSpec((QB_SWA, d), lambda b, j: (b * ns_sw + j, 0)),
            _const_spec((1, d)),
            _const_spec(win_b.shape),
            pl.BlockSpec((1, 1, QB_SWA), lambda b, j: (b, 0, j)),
            _const_spec((B_ROPE_HALF, QB_SWA)),
            _const_spec((B_HD, QB_SWA)),
            pl.BlockSpec((1, QB_SWA, B_KV_HEADS * B_K_PAD), lambda b, j: (b, j, 0)),
            pl.BlockSpec((1, B_WINDOW, B_KV_HEADS * B_K_PAD), lambda b, j: (b, prev_blk(b, j), 0)),
            pl.BlockSpec((1, B_KV_HEADS, B_V_PAD, QB_SWA), lambda b, j: (b, 0, 0, j)),
            pl.BlockSpec((1, B_KV_HEADS, B_V_PAD, B_WINDOW), lambda b, j: (b, 0, 0, prev_blk(b, j))),
        ],
        out_specs=pl.BlockSpec((QB_SWA, q_dim), lambda b, j: (b * ns_sw + j, 0)),
        out_shape=jax.ShapeDtypeStruct((t, q_dim), BF16),
        scratch_shapes=[pltpu.VMEM((QB_SWA, d), BF16),
                        pltpu.VMEM((blk_per_step, q_dim, B_WINDOW), BF16),
                        pltpu.VMEM((QB_SWA, q_dim), BF16),
                        pltpu.VMEM((blk_per_step, q_dim, B_WINDOW), F32)],
        compiler_params=_params(2),
        name="swa_attn",
    )(b_sinks[0], x1, b_norm[0].reshape(1, d), win_b, pos_row,
      jnp.broadcast_to(inv_b[:, None], (B_ROPE_HALF, QB_SWA)),
      jnp.broadcast_to(b_q_head_norm[0][:, None], (B_HD, QB_SWA)),
      k_b, k_b, vt_b, vt_b)

    n_out = t // TM_OUT
    out = pl.pallas_call(
        _swa_out_kernel,
        grid=(n_out,),
        in_specs=[
            pl.BlockSpec((TM_OUT, q_dim), lambda i: (i, 0)),
            pl.BlockSpec((TM_OUT, d), lambda i: (i, 0)),
            _const_spec(wo_b.shape),
        ],
        out_specs=pl.BlockSpec((TM_OUT, d), lambda i: (i, 0)),
        out_shape=jax.ShapeDtypeStruct((t, d), F32),
        compiler_params=_params(1),
        name="swa_out",
    )(og_b, x1, wo_b)
    return out.reshape(bsz, seq, d)
```
